```python
import math
import jax, jax.numpy as jnp
from jax import lax
import numpy as np

D_MODEL = 2048
BATCH = 4
SEQ = 4096
DEPTH = 2

HEAD_DIM = 128
WIDTH_A = D_MODEL // 2
WIDTH_B = D_MODEL // 2
N_HEADS_A = WIDTH_A // HEAD_DIM
N_HEADS_B = WIDTH_B // HEAD_DIM
CHUNK = 128
Q_BLOCK = 128
SSM_WIDTH = D_MODEL // 2
SSM_GROUP = 16
SSM_GROUPS = SSM_WIDTH // SSM_GROUP
SSM_STATE = 64
EPS = 1e-6
DT_MIN = 1e-3
DT_MAX = 1e-1

kernel_name = "hybrid_sgu_stickbreak_s5_adaln"


def rms_norm(x, g):
    xf = x.astype(jnp.float32)
    y = xf * lax.rsqrt(jnp.mean(xf * xf, axis=-1, keepdims=True) + EPS)
    return (y * g.astype(jnp.float32)).astype(x.dtype)


def spatial_gating(u, v, norm_g, w_s, b_s):
    bsz, l, _ = v.shape
    n_chunks = l // CHUNK
    vh = v.reshape(bsz, n_chunks, CHUNK, N_HEADS_A, HEAD_DIM)
    vh = rms_norm(vh, norm_g.reshape(N_HEADS_A, HEAD_DIM))
    causal = jnp.tril(jnp.ones((CHUNK, CHUNK), dtype=bool))
    w = jnp.where(causal[None], w_s, 0.0).astype(vh.dtype)
    s = jnp.einsum('hts,bnshd->bnthd', w, vh) + b_s.T.astype(vh.dtype)[None, None, :, :, None]
    return u * s.reshape(bsz, l, WIDTH_A)


def stick_breaking(q, k, v):
    bsz, l, h, dh = q.shape
    n_blocks = l // Q_BLOCK
    qb = q.reshape(bsz, n_blocks, Q_BLOCK, h, dh).transpose(1, 0, 3, 2, 4)
    kt = k.transpose(0, 2, 1, 3)
    vt = v.transpose(0, 2, 1, 3)
    scale = 1.0 / math.sqrt(dh)
    k_pos = jnp.arange(l)

    def block(args):
        q_blk, blk = args
        q_pos = blk * Q_BLOCK + jnp.arange(Q_BLOCK)
        mask = k_pos[None, :] < q_pos[:, None]
        z = jnp.einsum('bhqd,bhkd->bhqk', q_blk, kt).astype(jnp.float32) * scale
        log_beta = jax.nn.log_sigmoid(z)
        log_keep = jnp.where(mask, jax.nn.log_sigmoid(-z), 0.0)
        later = lax.cumsum(log_keep, axis=3, reverse=True) - log_keep
        w = jnp.where(mask, jnp.exp(log_beta + later), 0.0)
        return jnp.einsum('bhqk,bhkd->bhqd', w.astype(vt.dtype), vt)

    out = lax.map(block, (qb, jnp.arange(n_blocks)))
    return out.transpose(1, 0, 3, 2, 4).reshape(bsz, l, h * dh)


def s5_ssm(u, lam_re, lam_im, b_re, b_im, c_re, c_im, d_skip, log_dt):
    f32 = jnp.float32
    bsz, l, _ = u.shape
    uf = u.astype(f32).reshape(bsz, l, SSM_GROUPS, SSM_GROUP)
    dt = jnp.exp(log_dt.astype(f32))[:, None]
    lr = lam_re.astype(f32)
    li = lam_im.astype(f32)
    mag = jnp.exp(lr * dt)
    a_re = mag * jnp.cos(li * dt)
    a_im = mag * jnp.sin(li * dt)
    den = lr * lr + li * li
    nr = a_re - 1.0
    coef_re = (nr * lr + a_im * li) / den
    coef_im = (a_im * lr - nr * li) / den
    br = b_re.astype(f32)
    bi = b_im.astype(f32)
    bb_re = coef_re[..., None] * br - coef_im[..., None] * bi
    bb_im = coef_re[..., None] * bi + coef_im[..., None] * br
    bu_re = jnp.einsum('gpc,blgc->blgp', bb_re, uf)
    bu_im = jnp.einsum('gpc,blgc->blgp', bb_im, uf)
    a_re_t = jnp.broadcast_to(a_re, (1, l) + a_re.shape)
    a_im_t = jnp.broadcast_to(a_im, (1, l) + a_im.shape)

    def combine(e1, e2):
        a1r, a1i, b1r, b1i = e1
        a2r, a2i, b2r, b2i = e2
        return (a2r * a1r - a2i * a1i,
                a2r * a1i + a2i * a1r,
                a2r * b1r - a2i * b1i + b2r,
                a2r * b1i + a2i * b1r + b2i)

    _, _, h_re, h_im = lax.associative_scan(combine, (a_re_t, a_im_t, bu_re, bu_im), axis=1)
    y = (jnp.einsum('gcp,blgp->blgc', c_re.astype(f32), h_re)
         - jnp.einsum('gcp,blgp->blgc', c_im.astype(f32), h_im))
    y = y.reshape(bsz, l, SSM_WIDTH) + d_skip.astype(f32) * u.astype(f32)
    return y.astype(u.dtype)


def ab_mixer(h, w_in, w_out, sgu_norm_g, sgu_w, sgu_b):
    bsz, l, _ = h.shape
    proj = h @ w_in
    cuts = np.cumsum([WIDTH_A, WIDTH_A, WIDTH_A, WIDTH_B, WIDTH_B, WIDTH_B]).tolist()
    a_u, a_v, a_z, q, k, v, b_z = jnp.split(proj, cuts, axis=-1)
    out_a = spatial_gating(jax.nn.gelu(a_u), jax.nn.gelu(a_v), sgu_norm_g, sgu_w, sgu_b)
    out_a = out_a * jax.nn.silu(a_z)
    shp = (bsz, l, N_HEADS_B, HEAD_DIM)
    out_b = stick_breaking(q.reshape(shp), k.reshape(shp), v.reshape(shp)) * jax.nn.silu(b_z)
    return jnp.concatenate([out_a, out_b], axis=-1) @ w_out


def ssm_mixer(h, w_in, w_out, lam_re, lam_im, b_re, b_im, c_re, c_im, d_skip, log_dt, w_glu, b_glu):
    proj = h @ w_in
    u, z = jnp.split(proj, 2, axis=-1)
    y = s5_ssm(u, lam_re, lam_im, b_re, b_im, c_re, c_im, d_skip, log_dt)
    g = jax.nn.gelu(y)
    y = g * jax.nn.sigmoid(g @ w_glu + b_glu)
    return (y * jax.nn.silu(z)) @ w_out


def setup_inputs(seed: int = 0) -> dict:
    key = jax.random.key(seed)
    ks = jax.random.split(key, 24)
    n_even = (DEPTH + 1) // 2
    n_odd = DEPTH // 2
    d = D_MODEL
    nrm = jax.random.normal
    w_in_ab_cols = 3 * WIDTH_A + 4 * WIDTH_B
    log_dt = jax.random.uniform(ks[20], (n_odd, SSM_GROUPS), minval=math.log(DT_MIN), maxval=math.log(DT_MAX))
    n_idx = jnp.arange(SSM_STATE, dtype=jnp.float32)
    return {
        "x": nrm(ks[0], (BATCH, SEQ, d)),
        "c": nrm(ks[1], (BATCH, d)),
        "ln_pre_g": 1.0 + 0.02 * nrm(ks[2], (DEPTH, d)),
        "ln_post_g": 1.0 + 0.02 * nrm(ks[3], (DEPTH, d)),
        "w_mod": nrm(ks[4], (DEPTH, d, 3 * d)) * d ** -0.5,
        "b_mod": 0.02 * nrm(ks[5], (DEPTH, 3 * d)),
        "w_in_ab": nrm(ks[6], (n_even, d, w_in_ab_cols)) * d ** -0.5,
        "w_out_ab": nrm(ks[7], (n_even, WIDTH_A + WIDTH_B, d)) * (WIDTH_A + WIDTH_B) ** -0.5,
        "sgu_norm_g": 1.0 + 0.02 * nrm(ks[8], (n_even, WIDTH_A)),
        "sgu_w": nrm(ks[9], (n_even, N_HEADS_A, CHUNK, CHUNK)) * CHUNK ** -0.5,
        "sgu_b": 1.0 + 0.02 * nrm(ks[10], (n_even, N_HEADS_A, CHUNK)),
        "w_in_ssm": nrm(ks[11], (n_odd, d, 2 * SSM_WIDTH)) * d ** -0.5,
        "w_out_ssm": nrm(ks[12], (n_odd, SSM_WIDTH, d)) * SSM_WIDTH ** -0.5,
        "lam_re": -0.5 + 0.01 * nrm(ks[13], (n_odd, SSM_GROUPS, SSM_STATE)),
        "lam_im": math.pi * n_idx + 0.01 * nrm(ks[14], (n_odd, SSM_GROUPS, SSM_STATE)),
        "b_re": nrm(ks[15], (n_odd, SSM_GROUPS, SSM_STATE, SSM_GROUP)) * (2 * SSM_GROUP) ** -0.5,
        "b_im": nrm(ks[16], (n_odd, SSM_GROUPS, SSM_STATE, SSM_GROUP)) * (2 * SSM_GROUP) ** -0.5,
        "c_re": nrm(ks[17], (n_odd, SSM_GROUPS, SSM_GROUP, SSM_STATE)) * (2 * SSM_STATE) ** -0.5,
        "c_im": nrm(ks[18], (n_odd, SSM_GROUPS, SSM_GROUP, SSM_STATE)) * (2 * SSM_STATE) ** -0.5,
        "d_skip": nrm(ks[19], (n_odd, SSM_WIDTH)),
        "log_dt": log_dt,
        "w_glu": nrm(ks[21], (n_odd, SSM_WIDTH, SSM_WIDTH)) * SSM_WIDTH ** -0.5,
        "b_glu": 0.02 * nrm(ks[22], (n_odd, SSM_WIDTH)),
    }


def reference(x, c, ln_pre_g, ln_post_g, w_mod, b_mod, w_in_ab, w_out_ab, sgu_norm_g, sgu_w, sgu_b,
              w_in_ssm, w_out_ssm, lam_re, lam_im, b_re, b_im, c_re, c_im, d_skip, log_dt, w_glu, b_glu):
    cond = jax.nn.silu(c)
    for layer in range(DEPTH):
        mod = cond @ w_mod[layer] + b_mod[layer]
        shift, scale, gate = jnp.split(mod[:, None, :], 3, axis=-1)
        h = rms_norm(x, ln_pre_g[layer]) * (1.0 + scale) + shift
        i = layer // 2
        if layer % 2 == 0:
            y = ab_mixer(h, w_in_ab[i], w_out_ab[i], sgu_norm_g[i], sgu_w[i], sgu_b[i])
        else:
            y = ssm_mixer(h, w_in_ssm[i], w_out_ssm[i], lam_re[i], lam_im[i], b_re[i], b_im[i],
                          c_re[i], c_im[i], d_skip[i], log_dt[i], w_glu[i], b_glu[i])
        x = x + (gate * rms_norm(y, ln_post_g[layer])).astype(x.dtype)
    return x
```

```python
import functools
import math

import jax
import jax.numpy as jnp
from jax import lax
from jax.experimental import pallas as pl
from jax.experimental.pallas import tpu as pltpu

F32 = jnp.float32
BF16 = jnp.bfloat16

EPS = 1e-6
LANES = 128
HEAD_DIM = 128
SGU_CHUNK = 128
SSM_GROUP = 16
SSM_STATE = 64
SSM_T = 8
GROUPS_PER_SLAB = LANES // SSM_GROUP
SLAB_STATE = GROUPS_PER_SLAB * SSM_STATE
VMEM_LIMIT = 56 * 1024 * 1024


def _params(sem, vmem=VMEM_LIMIT):
    return pltpu.CompilerParams(dimension_semantics=sem, vmem_limit_bytes=vmem)


def _rms(y, g):
    ms = jnp.mean(y * y, axis=-1, keepdims=True)
    return y * lax.rsqrt(ms + EPS) * g


def _mod_kernel(c_ref, w_ref, b_ref, o_ref):
    cond = jax.nn.silu(c_ref[...])
    o_ref[...] = jnp.dot(cond.astype(BF16), w_ref[...].astype(BF16),
                         preferred_element_type=F32) + b_ref[...]


def _adaln_mod(c_pad, w_mod, b_mod, tn=512):
    depth, d, n = w_mod.shape
    rows = c_pad.shape[0]
    return pl.pallas_call(
        _mod_kernel,
        grid=(depth, n // tn),
        in_specs=[
            pl.BlockSpec((rows, d), lambda l, j: (0, 0)),
            pl.BlockSpec((None, d, tn), lambda l, j: (l, 0, j)),
            pl.BlockSpec((None, 1, tn), lambda l, j: (l, 0, j)),
        ],
        out_specs=pl.BlockSpec((None, rows, tn), lambda l, j: (l, 0, j)),
        out_shape=jax.ShapeDtypeStruct((depth, rows, n), F32),
        compiler_params=_params(("parallel", "parallel")),
        name="adaln_mod",
    )(c_pad, w_mod, b_mod.reshape(depth, 1, n))


def _prenorm(x_ref, g_ref, sc_ref, sh_ref):
    y = _rms(x_ref[...], g_ref[...])
    return (y * (1.0 + sc_ref[...]) + sh_ref[...]).astype(BF16)


def _inproj_kernel(x_ref, g_ref, sc_ref, sh_ref, w_ref, o_ref, h_scr):
    @pl.when(pl.program_id(2) == 0)
    def _():
        h_scr[...] = _prenorm(x_ref, g_ref, sc_ref, sh_ref)

    o_ref[...] = jnp.dot(h_scr[...], w_ref[...],
                         preferred_element_type=F32).astype(o_ref.dtype)


def _inproj(x, g, scale, shift, w, tm=512, tn=1024):
    b, l, d = x.shape
    n = w.shape[1]
    tn = math.gcd(n, tn)
    return pl.pallas_call(
        _inproj_kernel,
        grid=(b, l // tm, n // tn),
        in_specs=[
            pl.BlockSpec((None, tm, d), lambda bi, i, j: (bi, i, 0)),
            pl.BlockSpec((1, d), lambda bi, i, j: (0, 0)),
            pl.BlockSpec((None, 1, d), lambda bi, i, j: (bi, 0, 0)),
            pl.BlockSpec((None, 1, d), lambda bi, i, j: (bi, 0, 0)),
            pl.BlockSpec((d, tn), lambda bi, i, j: (0, j)),
        ],
        out_specs=pl.BlockSpec((None, tm, tn), lambda bi, i, j: (bi, i, j)),
        out_shape=jax.ShapeDtypeStruct((b, l, n), BF16),
        scratch_shapes=[pltpu.VMEM((tm, d), BF16)],
        compiler_params=_params(("parallel", "parallel", "arbitrary")),
        name="inproj_ab",
    )(x, g.reshape(1, d), scale.reshape(b, 1, d), shift.reshape(b, 1, d), w)


def _inproj_ssm_kernel(x_ref, g_ref, sc_ref, sh_ref, w_ref, u_ref, z_ref, *, width):
    h = _prenorm(x_ref, g_ref, sc_ref, sh_ref)
    acc = jnp.dot(h, w_ref[...], preferred_element_type=F32)
    tm = acc.shape[0]
    for j in range(width // LANES):
        u_ref[:, j] = acc[:, j * LANES:(j + 1) * LANES].reshape(tm // SSM_T, SSM_T, LANES)
    z_ref[...] = acc[:, width:].astype(z_ref.dtype)


def _inproj_ssm(x, g, scale, shift, w, tm=512):
    b, l, d = x.shape
    width = w.shape[1] // 2
    slabs = width // LANES
    kern = functools.partial(_inproj_ssm_kernel, width=width)
    return pl.pallas_call(
        kern,
        grid=(b, l // tm),
        in_specs=[
            pl.BlockSpec((None, tm, d), lambda bi, i: (bi, i, 0)),
            pl.BlockSpec((1, d), lambda bi, i: (0, 0)),
            pl.BlockSpec((None, 1, d), lambda bi, i: (bi, 0, 0)),
            pl.BlockSpec((None, 1, d), lambda bi, i: (bi, 0, 0)),
            pl.BlockSpec((d, 2 * width), lambda bi, i: (0, 0)),
        ],
        out_specs=[
            pl.BlockSpec((tm // SSM_T, None, slabs, SSM_T, LANES),
                         lambda bi, i: (i, bi, 0, 0, 0)),
            pl.BlockSpec((None, tm, width), lambda bi, i: (bi, i, 0)),
        ],
        out_shape=[
            jax.ShapeDtypeStruct((l // SSM_T, b, slabs, SSM_T, LANES), F32),
            jax.ShapeDtypeStruct((b, l, width), BF16),
        ],
        compiler_params=_params(("parallel", "parallel")),
        name="inproj_ssm",
    )(x, g.reshape(1, d), scale.reshape(b, 1, d), shift.reshape(b, 1, d), w)


def _sgu_kernel(u_ref, v_ref, z_ref, w_ref, b_ref, g_ref, o_ref, *, heads, chunks):
    row = lax.broadcasted_iota(jnp.int32, (SGU_CHUNK, SGU_CHUNK), 0)
    col = lax.broadcasted_iota(jnp.int32, (SGU_CHUNK, SGU_CHUNK), 1)
    causal = col <= row
    for h in range(heads):
        w = jnp.where(causal, w_ref[h], 0.0).astype(BF16)
        bias = b_ref[h]
        gain = g_ref[:, h * HEAD_DIM:(h + 1) * HEAD_DIM]
        for c in range(chunks):
            rs = slice(c * SGU_CHUNK, (c + 1) * SGU_CHUNK)
            cs = slice(h * HEAD_DIM, (h + 1) * HEAD_DIM)
            v = jax.nn.gelu(v_ref[rs, cs].astype(F32))
            vn = _rms(v, gain)
            s = jnp.dot(w, vn.astype(BF16), preferred_element_type=F32) + bias
            u = jax.nn.gelu(u_ref[rs, cs].astype(F32))
            z = jax.nn.silu(z_ref[rs, cs].astype(F32))
            o_ref[rs, cs] = (u * s * z).astype(o_ref.dtype)


def _sgu(proj, sgu_w, sgu_b, sgu_g, width, tm=512):
    b, l, _ = proj.shape
    heads = width // HEAD_DIM

    def col_spec(k):
        return pl.BlockSpec((None, tm, width), lambda bi, i: (bi, i, k))

    kern = functools.partial(_sgu_kernel, heads=heads, chunks=tm // SGU_CHUNK)
    return pl.pallas_call(
        kern,
        grid=(b, l // tm),
        in_specs=[
            col_spec(0), col_spec(1), col_spec(2),
            pl.BlockSpec((heads, SGU_CHUNK, SGU_CHUNK), lambda bi, i: (0, 0, 0)),
            pl.BlockSpec((heads, SGU_CHUNK, 1), lambda bi, i: (0, 0, 0)),
            pl.BlockSpec((1, width), lambda bi, i: (0, 0)),
        ],
        out_specs=pl.BlockSpec((None, tm, width), lambda bi, i: (bi, i, 0)),
        out_shape=jax.ShapeDtypeStruct((b, l, width), BF16),
        compiler_params=_params(("parallel", "parallel")),
        name="sgu",
    )(proj, proj, proj, sgu_w, sgu_b.reshape(heads, SGU_CHUNK, 1), sgu_g.reshape(1, width))


def _attn_kernel(q_ref, k_ref, v_ref, bz_ref, o_ref, acc_scr, car_scr, *, tq, tk, scale):
    qi = pl.program_id(2)
    q = q_ref[...]
    q0 = qi * tq
    r_i = lax.broadcasted_iota(jnp.int32, (tk, tk), 0)
    c_i = lax.broadcasted_iota(jnp.int32, (tk, tk), 1)
    later_mat = (r_i > c_i).astype(BF16)
    q_pos = q0 + lax.broadcasted_iota(jnp.int32, (tq, tk), 0)
    k_off = lax.broadcasted_iota(jnp.int32, (tq, tk), 1)

    acc_scr[...] = jnp.zeros_like(acc_scr)
    car_scr[...] = jnp.zeros_like(car_scr)

    def block(j, masked):
        ks = pl.multiple_of(j * tk, tk)
        k = k_ref[pl.ds(ks, tk), :]
        v = v_ref[pl.ds(ks, tk), :]
        z = lax.dot_general(q, k, (((1,), (1,)), ((), ())),
                            preferred_element_type=F32) * scale
        soft = jnp.log(1.0 + jnp.exp(-jnp.abs(z)))
        log_beta = jnp.minimum(z, 0.0) - soft
        log_keep = log_beta - z
        if masked:
            mask = (ks + k_off) < q_pos
            log_keep = jnp.where(mask, log_keep, 0.0)
        hi = log_keep.astype(BF16)
        lo = (log_keep - hi.astype(F32)).astype(BF16)
        later = (jnp.dot(hi, later_mat, preferred_element_type=F32)
                 + jnp.dot(lo, later_mat, preferred_element_type=F32))
        carry = car_scr[...]
        w = jnp.exp(log_beta + later + carry)
        if masked:
            w = jnp.where(mask, w, 0.0)
        acc_scr[...] += jnp.dot(w.astype(BF16), v, preferred_element_type=F32)
        car_scr[...] = carry + later[:, :1] + log_keep[:, :1]

    n_diag = tq // tk
    for d in range(n_diag):
        block(qi * n_diag + (n_diag - 1 - d), True)

    n_off = qi * n_diag

    def body(it, _):
        block(n_off - 1 - it, False)
        return 0

    lax.fori_loop(0, n_off, body, 0)
    gate = jax.nn.silu(bz_ref[...].astype(F32))
    o_ref[...] = (acc_scr[...] * gate).astype(o_ref.dtype)


def _attention(proj, q_col, k_col, v_col, z_col, heads, tq=256, tk=128):
    b, l, _ = proj.shape
    kern = functools.partial(_attn_kernel, tq=tq, tk=tk, scale=1.0 / math.sqrt(HEAD_DIM))
    return pl.pallas_call(
        kern,
        grid=(b, heads, l // tq),
        in_specs=[
            pl.BlockSpec((None, tq, HEAD_DIM), lambda bi, h, i: (bi, i, q_col + h)),
            pl.BlockSpec((None, l, HEAD_DIM), lambda bi, h, i: (bi, 0, k_col + h)),
            pl.BlockSpec((None, l, HEAD_DIM), lambda bi, h, i: (bi, 0, v_col + h)),
            pl.BlockSpec((None, tq, HEAD_DIM), lambda bi, h, i: (bi, i, z_col + h)),
        ],
        out_specs=pl.BlockSpec((None, tq, HEAD_DIM), lambda bi, h, i: (bi, i, h)),
        out_shape=jax.ShapeDtypeStruct((b, l, heads * HEAD_DIM), BF16),
        scratch_shapes=[pltpu.VMEM((tq, HEAD_DIM), F32), pltpu.VMEM((tq, 1), F32)],
        compiler_params=_params(("parallel", "parallel", "parallel")),
        name="stickbreak_attn",
    )(proj, proj, proj, proj)


def _ab_out_kernel(a_ref, b_ref, wa_ref, wb_ref, x_ref, gate_ref, g_ref, o_ref):
    y = (jnp.dot(a_ref[...], wa_ref[...], preferred_element_type=F32)
         + jnp.dot(b_ref[...], wb_ref[...], preferred_element_type=F32))
    o_ref[...] = x_ref[...] + gate_ref[...] * _rms(y, g_ref[...])


def _ab_out(out_a, out_b, w_out, x, gate, g_post, tm=512):
    b, l, d = x.shape
    wa = out_a.shape[-1]
    wb = out_b.shape[-1]
    assert wa == wb
    return pl.pallas_call(
        _ab_out_kernel,
        grid=(b, l // tm),
        in_specs=[
            pl.BlockSpec((None, tm, wa), lambda bi, i: (bi, i, 0)),
            pl.BlockSpec((None, tm, wb), lambda bi, i: (bi, i, 0)),
            pl.BlockSpec((wa, d), lambda bi, i: (0, 0)),
            pl.BlockSpec((wb, d), lambda bi, i: (1, 0)),
            pl.BlockSpec((None, tm, d), lambda bi, i: (bi, i, 0)),
            pl.BlockSpec((None, 1, d), lambda bi, i: (bi, 0, 0)),
            pl.BlockSpec((1, d), lambda bi, i: (0, 0)),
        ],
        out_specs=pl.BlockSpec((None, tm, d), lambda bi, i: (bi, i, 0)),
        out_shape=jax.ShapeDtypeStruct((b, l, d), F32),
        compiler_params=_params(("parallel", "parallel")),
        name="ab_out",
    )(out_a, out_b, w_out, w_out, x, gate.reshape(b, 1, d), g_post.reshape(1, d))


def _ssm_prep_kernel(lr_ref, li_ref, ldt_ref, br_ref, bi_ref, cr_ref, ci_ref,
                     k_ref, bsr_ref, bsi_ref, csr_ref, csi_ref, atr_ref, ati_ref):
    lr = lr_ref[...]
    li = li_ref[...]
    dt = jnp.exp(ldt_ref[...])
    mag = jnp.exp(lr * dt)
    a_re = mag * jnp.cos(li * dt)
    a_im = mag * jnp.sin(li * dt)
    den = lr * lr + li * li
    nr = a_re - 1.0
    coef_re = (nr * lr + a_im * li) / den
    coef_im = (a_im * lr - nr * li) / den
    b_re = br_ref[...]
    b_im = bi_ref[...]
    bb_re = coef_re * b_re - coef_im * b_im
    bb_im = coef_re * b_im + coef_im * b_re
    c_re = cr_ref[...]
    c_im = ci_ref[...]
    rows = lr.shape[0]
    nt = (((1,), (1,)), ((), ()))
    hp = lax.Precision.HIGHEST
    p_re = jnp.ones_like(lr)
    p_im = jnp.zeros_like(lr)
    for d in range(SSM_T + 1):
        cd_re = c_re * p_re - c_im * p_im
        cd_im = c_re * p_im + c_im * p_re
        if d >= 1:
            csr_ref[d - 1] = cd_re
            csi_ref[d - 1] = -cd_im
        if d == SSM_T:
            atr_ref[...] = p_re
            ati_ref[...] = p_im
            break
        s = SSM_T - 1 - d
        bsr_ref[s] = bb_re * p_re - bb_im * p_im
        bsi_ref[s] = bb_re * p_im + bb_im * p_re
        for blk in range(rows // LANES):
            rs = slice(blk * LANES, (blk + 1) * LANES)
            k_ref[d, rs, :] = (
                lax.dot_general(cd_re[rs], bb_re[rs], nt, precision=hp,
                                preferred_element_type=F32)
                - lax.dot_general(cd_im[rs], bb_im[rs], nt, precision=hp,
                                  preferred_element_type=F32))
        p_re, p_im = p_re * a_re - p_im * a_im, p_re * a_im + p_im * a_re


def _ssm_prep(lam_re, lam_im, log_dt, b_re, b_im, c_re, c_im):
    g, p = lam_re.shape
    rows = g * SSM_GROUP
    rep = lambda a: jnp.repeat(a, SSM_GROUP, axis=0)
    b_t = lambda a: jnp.transpose(a, (0, 2, 1)).reshape(rows, p)
    ldt = jnp.broadcast_to(rep(log_dt.reshape(g, 1)), (rows, p))
    full = lambda: pl.BlockSpec((rows, p), lambda: (0, 0))
    stack = lambda n: pl.BlockSpec((SSM_T, rows, n), lambda: (0, 0, 0))
    return pl.pallas_call(
        _ssm_prep_kernel,
        in_specs=[full() for _ in range(7)],
        out_specs=[stack(LANES), stack(p), stack(p), stack(p), stack(p), full(), full()],
        out_shape=[
            jax.ShapeDtypeStruct((SSM_T, rows, LANES), F32),
            jax.ShapeDtypeStruct((SSM_T, rows, p), F32),
            jax.ShapeDtypeStruct((SSM_T, rows, p), F32),
            jax.ShapeDtypeStruct((SSM_T, rows, p), F32),
            jax.ShapeDtypeStruct((SSM_T, rows, p), F32),
            jax.ShapeDtypeStruct((rows, p), F32),
            jax.ShapeDtypeStruct((rows, p), F32),
        ],
        compiler_params=pltpu.CompilerParams(vmem_limit_bytes=VMEM_LIMIT),
        name="ssm_prep",
    )(rep(lam_re), rep(lam_im), ldt, b_t(b_re), b_t(b_im),
      c_re.reshape(rows, p), c_im.reshape(rows, p))


def _ssm_assemble(kf, bsr, bsi, csr, csi, atr, ati, d_skip):
    t = SSM_T
    gs = GROUPS_PER_SLAB
    c = SSM_GROUP
    p = SSM_STATE
    rows = kf.shape[1]
    slabs = rows // LANES
    eye = jnp.eye(gs, dtype=bool)
    k6 = kf.reshape(t, slabs, gs, c, gs, c)
    idx = jnp.arange(gs)
    kd = k6[:, :, idx, :, idx, :]
    s_i = jnp.arange(t)[:, None]
    t_i = jnp.arange(t)[None, :]
    lag = jnp.clip(t_i - s_i, 0, t - 1)
    kg = kd[:, lag]
    kg = jnp.where((t_i >= s_i)[None, :, :, None, None, None], kg, 0.0)
    kg = jnp.transpose(kg, (3, 1, 0, 5, 2, 4))
    toep = jnp.where(eye[None, None, :, None, None, :, None],
                     kg[:, :, :, :, :, None, :], 0.0)
    toep = toep.reshape(slabs, t * LANES, t * LANES)
    bs = jnp.stack([bsr, bsi], axis=-2)
    bs = bs.reshape(t, slabs, gs, c, 2, p)
    bs = jnp.transpose(bs, (1, 0, 2, 3, 4, 5))
    bs = jnp.where(eye[None, None, :, None, None, :, None],
                   bs[:, :, :, :, :, None, :], 0.0)
    bs = bs.reshape(slabs, t * LANES, 2 * gs * p)
    w_in = jnp.concatenate([toep, bs], axis=-1).astype(BF16)
    cs = jnp.stack([csr, csi], axis=0)
    cs = cs.reshape(2, t, slabs, gs, c, p)
    cs = jnp.transpose(cs, (2, 0, 3, 5, 1, 4))
    cs = jnp.where(eye[None, None, :, None, None, :, None],
                   cs[:, :, :, :, :, None, :], 0.0)
    cs = cs.reshape(slabs, 2 * gs * p, t * LANES).astype(BF16)
    a_t = jnp.concatenate([atr[::c].reshape(slabs, 1, gs * p),
                           ati[::c].reshape(slabs, 1, gs * p)], axis=-1)
    dsk = jnp.tile(d_skip.reshape(slabs, 1, LANES), (1, 1, t))
    return w_in, cs, a_t, dsk


def _ssm_kernel(u_ref, w_ref, cs_ref, at_ref, dsk_ref, y_ref, s_scr, hp_scr, h_scr,
                *, batch, n_chunks):
    ns = SLAB_STATE
    out_w = SSM_T * LANES

    @pl.when(pl.program_id(1) == 0)
    def _():
        h_scr[...] = jnp.zeros_like(h_scr)

    u = u_ref[...]
    r = jnp.dot(u.astype(BF16), w_ref[...], preferred_element_type=F32)
    s_scr[...] = r[:, out_w:]
    a_re = jnp.broadcast_to(at_ref[:, :ns], (batch, ns))
    a_im = jnp.broadcast_to(at_ref[:, ns:], (batch, ns))

    sub = 8
    per_tile = sub // batch

    def step(k, carry):
        h_re, h_im = carry
        r0 = pl.multiple_of(k * sub, sub)
        s_re = s_scr[pl.ds(r0, sub), :ns]
        s_im = s_scr[pl.ds(r0, sub), ns:]
        prev_re, prev_im = [], []
        for c in range(per_tile):
            rows = slice(c * batch, (c + 1) * batch)
            prev_re.append(h_re)
            prev_im.append(h_im)
            h_re, h_im = (a_re * h_re - a_im * h_im + s_re[rows],
                          a_re * h_im + a_im * h_re + s_im[rows])
        hp_scr[pl.ds(r0, sub), :ns] = jnp.concatenate(prev_re, axis=0)
        hp_scr[pl.ds(r0, sub), ns:] = jnp.concatenate(prev_im, axis=0)
        return h_re, h_im

    h_re, h_im = lax.fori_loop(0, n_chunks // per_tile, step,
                               (h_scr[:, :ns], h_scr[:, ns:]))
    h_scr[:, :ns] = h_re
    h_scr[:, ns:] = h_im
    y = (r[:, :out_w]
         + jnp.dot(hp_scr[...].astype(BF16), cs_ref[...], preferred_element_type=F32)
         + dsk_ref[...] * u)
    y_ref[...] = y


def _ssm(u5, w_in, cs, a_t, dsk, n_rows=512):
    lc, b, slabs, t, _ = u5.shape
    rows = lc * b
    width = t * LANES
    u2 = u5.reshape(rows, slabs * width)
    n_rows = min(n_rows, rows)
    kern = functools.partial(_ssm_kernel, batch=b, n_chunks=n_rows // b)
    y2 = pl.pallas_call(
        kern,
        grid=(slabs, rows // n_rows),
        in_specs=[
            pl.BlockSpec((n_rows, width), lambda j, i: (i, j)),
            pl.BlockSpec((None, width, width + 2 * SLAB_STATE), lambda j, i: (j, 0, 0)),
            pl.BlockSpec((None, 2 * SLAB_STATE, width), lambda j, i: (j, 0, 0)),
            pl.BlockSpec((None, 1, 2 * SLAB_STATE), lambda j, i: (j, 0, 0)),
            pl.BlockSpec((None, 1, width), lambda j, i: (j, 0, 0)),
        ],
        out_specs=pl.BlockSpec((n_rows, width), lambda j, i: (i, j)),
        out_shape=jax.ShapeDtypeStruct((rows, slabs * width), F32),
        scratch_shapes=[
            pltpu.VMEM((n_rows, 2 * SLAB_STATE), F32),
            pltpu.VMEM((n_rows, 2 * SLAB_STATE), F32),
            pltpu.VMEM((b, 2 * SLAB_STATE), F32),
        ],
        compiler_params=_params(("parallel", "arbitrary")),
        name="ssm_scan",
    )(u2, w_in, cs, a_t, dsk)
    return y2.reshape(lc, b, slabs, t, LANES)


def _glu_out_kernel(y_ref, z_ref, wg_ref, bg_ref, wo_ref, x_ref, gate_ref, g_ref, o_ref):
    slabs = y_ref.shape[1]
    tm = z_ref.shape[0]
    y = jnp.concatenate([y_ref[:, j].reshape(tm, LANES) for j in range(slabs)], axis=-1)
    g = jax.nn.gelu(y)
    t = jnp.dot(g.astype(BF16), wg_ref[...], preferred_element_type=F32) + bg_ref[...]
    y2 = g * jax.nn.sigmoid(t) * jax.nn.silu(z_ref[...].astype(F32))
    o = jnp.dot(y2.astype(BF16), wo_ref[...], preferred_element_type=F32)
    o_ref[...] = x_ref[...] + gate_ref[...] * _rms(o, g_ref[...])


def _glu_out(y5, z, w_glu, b_glu, w_out, x, gate, g_post, tm=512):
    b, l, d = x.shape
    _, _, slabs, t, _ = y5.shape
    width = slabs * LANES
    return pl.pallas_call(
        _glu_out_kernel,
        grid=(b, l // tm),
        in_specs=[
            pl.BlockSpec((tm // t, None, slabs, t, LANES), lambda bi, i: (i, bi, 0, 0, 0)),
            pl.BlockSpec((None, tm, width), lambda bi, i: (bi, i, 0)),
            pl.BlockSpec((width, width), lambda bi, i: (0, 0)),
            pl.BlockSpec((1, width), lambda bi, i: (0, 0)),
            pl.BlockSpec((width, d), lambda bi, i: (0, 0)),
            pl.BlockSpec((None, tm, d), lambda bi, i: (bi, i, 0)),
            pl.BlockSpec((None, 1, d), lambda bi, i: (bi, 0, 0)),
            pl.BlockSpec((1, d), lambda bi, i: (0, 0)),
        ],
        out_specs=pl.BlockSpec((None, tm, d), lambda bi, i: (bi, i, 0)),
        out_shape=jax.ShapeDtypeStruct((b, l, d), F32),
        compiler_params=_params(("parallel", "parallel")),
        name="glu_out",
    )(y5, z, w_glu, b_glu.reshape(1, width), w_out, x, gate.reshape(b, 1, d),
      g_post.reshape(1, d))


def _ab_layer(x, mod, g_pre, g_post, w_in, w_out, sgu_g, sgu_w, sgu_b):
    d = x.shape[-1]
    shift, scale, gate = mod[:, :d], mod[:, d:2 * d], mod[:, 2 * d:]
    wa = d // 2
    heads = wa // HEAD_DIM
    proj = _inproj(x, g_pre, scale, shift, w_in.astype(BF16))
    out_a = _sgu(proj, sgu_w, sgu_b, sgu_g, wa)
    blk = lambda cols: cols // HEAD_DIM
    out_b = _attention(proj, blk(3 * wa), blk(4 * wa), blk(5 * wa), blk(6 * wa), heads)
    return _ab_out(out_a, out_b, w_out.astype(BF16), x, gate, g_post)


def _ssm_layer(x, mod, g_pre, g_post, w_in, w_out, lam_re, lam_im, b_re, b_im, c_re, c_im,
               d_skip, log_dt, w_glu, b_glu):
    d = x.shape[-1]
    shift, scale, gate = mod[:, :d], mod[:, d:2 * d], mod[:, 2 * d:]
    u5, z = _inproj_ssm(x, g_pre, scale, shift, w_in.astype(BF16))
    prep = _ssm_prep(lam_re, lam_im, log_dt, b_re, b_im, c_re, c_im)
    y5 = _ssm(u5, *_ssm_assemble(*prep, d_skip))
    return _glu_out(y5, z, w_glu.astype(BF16), b_glu, w_out.astype(BF16), x, gate, g_post)


def kernel(x, c, ln_pre_g, ln_post_g, w_mod, b_mod, w_in_ab, w_out_ab, sgu_norm_g, sgu_w, sgu_b,
           w_in_ssm, w_out_ssm, lam_re, lam_im, b_re, b_im, c_re, c_im, d_skip, log_dt,
           w_glu, b_glu):
    depth = w_mod.shape[0]
    batch = x.shape[0]
    pad = (-batch) % 8
    c_pad = jnp.pad(c, ((0, pad), (0, 0)))
    mod = _adaln_mod(c_pad, w_mod, b_mod)[:, :batch]
    for layer in range(depth):
        i = layer // 2
        if layer % 2 == 0:
            x = _ab_layer(x, mod[layer], ln_pre_g[layer], ln_post_g[layer], w_in_ab[i],
                          w_out_ab[i], sgu_norm_g[i], sgu_w[i], sgu_b[i])
        else:
            x = _ssm_layer(x, mod[layer], ln_pre_g[layer], ln_post_g[layer], w_in_ssm[i],
                           w_out_ssm[i], lam_re[i], lam_im[i], b_re[i], b_im[i], c_re[i],
                           c_im[i], d_skip[i], log_dt[i], w_glu[i], b_glu[i])
    return x
```

```python
import functools
import math

import jax
import jax.numpy as jnp
from jax import lax
from jax.experimental import pallas as pl
from jax.experimental.pallas import tpu as pltpu

F32 = jnp.float32
BF16 = jnp.bfloat16

EPS = 1e-6
LANES = 128
HEAD_DIM = 128
SGU_CHUNK = 128
SSM_GROUP = 16
SSM_STATE = 64
SSM_T = 8
GROUPS_PER_SLAB = LANES // SSM_GROUP
SLAB_STATE = GROUPS_PER_SLAB * SSM_STATE
VMEM_LIMIT = 56 * 1024 * 1024


def _params(sem, vmem=VMEM_LIMIT):
    return pltpu.CompilerParams(dimension_semantics=sem, vmem_limit_bytes=vmem)


def _rms(y, g):
    ms = jnp.mean(y * y, axis=-1, keepdims=True)
    return y * lax.rsqrt(ms + EPS) * g


def _mod_kernel(c_ref, w_ref, b_ref, o_ref):
    cond = jax.nn.silu(c_ref[...])
    o_ref[...] = jnp.dot(cond.astype(BF16), w_ref[...].astype(BF16),
                         preferred_element_type=F32) + b_ref[...]


def _adaln_mod(c_pad, w_mod, b_mod, tn=512):
    depth, d, n = w_mod.shape
    rows = c_pad.shape[0]
    return pl.pallas_call(
        _mod_kernel,
        grid=(depth, n // tn),
        in_specs=[
            pl.BlockSpec((rows, d), lambda l, j: (0, 0)),
            pl.BlockSpec((None, d, tn), lambda l, j: (l, 0, j)),
            pl.BlockSpec((None, 1, tn), lambda l, j: (l, 0, j)),
        ],
        out_specs=pl.BlockSpec((None, rows, tn), lambda l, j: (l, 0, j)),
        out_shape=jax.ShapeDtypeStruct((depth, rows, n), F32),
        compiler_params=_params(("parallel", "parallel")),
        name="adaln_mod",
    )(c_pad, w_mod, b_mod.reshape(depth, 1, n))


def _prenorm(x_ref, g_ref, sc_ref, sh_ref):
    y = _rms(x_ref[...], g_ref[...])
    return (y * (1.0 + sc_ref[...]) + sh_ref[...]).astype(BF16)


def _inproj_kernel(x_ref, g_ref, sc_ref, sh_ref, w_ref, o_ref, h_scr):
    @pl.when(pl.program_id(2) == 0)
    def _():
        h_scr[...] = _prenorm(x_ref, g_ref, sc_ref, sh_ref)

    o_ref[...] = jnp.dot(h_scr[...], w_ref[...],
                         preferred_element_type=F32).astype(o_ref.dtype)


def _inproj(x, g, scale, shift, w, tm=512, tn=1024):
    b, l, d = x.shape
    n = w.shape[1]
    tn = math.gcd(n, tn)
    return pl.pallas_call(
        _inproj_kernel,
        grid=(b, l // tm, n // tn),
        in_specs=[
            pl.BlockSpec((None, tm, d), lambda bi, i, j: (bi, i, 0)),
            pl.BlockSpec((1, d), lambda bi, i, j: (0, 0)),
            pl.BlockSpec((None, 1, d), lambda bi, i, j: (bi, 0, 0)),
            pl.BlockSpec((None, 1, d), lambda bi, i, j: (bi, 0, 0)),
            pl.BlockSpec((d, tn), lambda bi, i, j: (0, j)),
        ],
        out_specs=pl.BlockSpec((None, tm, tn), lambda bi, i, j: (bi, i, j)),
        out_shape=jax.ShapeDtypeStruct((b, l, n), BF16),
        scratch_shapes=[pltpu.VMEM((tm, d), BF16)],
        compiler_params=_params(("parallel", "parallel", "arbitrary")),
        name="inproj_ab",
    )(x, g.reshape(1, d), scale.reshape(b, 1, d), shift.reshape(b, 1, d), w)


def _inproj_ssm_kernel(x_ref, g_ref, sc_ref, sh_ref, w_ref, u_ref, z_ref, *, width):
    h = _prenorm(x_ref, g_ref, sc_ref, sh_ref)
    acc = jnp.dot(h, w_ref[...], preferred_element_type=F32)
    tm = acc.shape[0]
    for j in range(width // LANES):
        u_ref[:, j] = acc[:, j * LANES:(j + 1) * LANES].reshape(tm // SSM_T, SSM_T, LANES)
    z_ref[...] = acc[:, width:].astype(z_ref.dtype)


def _inproj_ssm(x, g, scale, shift, w, tm=512):
    b, l, d = x.shape
    width = w.shape[1] // 2
    slabs = width // LANES
    kern = functools.partial(_inproj_ssm_kernel, width=width)
    return pl.pallas_call(
        kern,
        grid=(b, l // tm),
        in_specs=[
            pl.BlockSpec((None, tm, d), lambda bi, i: (bi, i, 0)),
            pl.BlockSpec((1, d), lambda bi, i: (0, 0)),
            pl.BlockSpec((None, 1, d), lambda bi, i: (bi, 0, 0)),
            pl.BlockSpec((None, 1, d), lambda bi, i: (bi, 0, 0)),
            pl.BlockSpec((d, 2 * width), lambda bi, i: (0, 0)),
        ],
        out_specs=[
            pl.BlockSpec((tm // SSM_T, None, slabs, SSM_T, LANES),
                         lambda bi, i: (i, bi, 0, 0, 0)),
            pl.BlockSpec((None, tm, width), lambda bi, i: (bi, i, 0)),
        ],
        out_shape=[
            jax.ShapeDtypeStruct((l // SSM_T, b, slabs, SSM_T, LANES), F32),
            jax.ShapeDtypeStruct((b, l, width), BF16),
        ],
        compiler_params=_params(("parallel", "parallel")),
        name="inproj_ssm",
    )(x, g.reshape(1, d), scale.reshape(b, 1, d), shift.reshape(b, 1, d), w)


def _sgu_kernel(u_ref, v_ref, z_ref, w_ref, b_ref, g_ref, o_ref, *, heads, chunks):
    row = lax.broadcasted_iota(jnp.int32, (SGU_CHUNK, SGU_CHUNK), 0)
    col = lax.broadcasted_iota(jnp.int32, (SGU_CHUNK, SGU_CHUNK), 1)
    causal = col <= row
    for h in range(heads):
        w = jnp.where(causal, w_ref[h], 0.0).astype(BF16)
        bias = b_ref[h]
        gain = g_ref[:, h * HEAD_DIM:(h + 1) * HEAD_DIM]
        for c in range(chunks):
            rs = slice(c * SGU_CHUNK, (c + 1) * SGU_CHUNK)
            cs = slice(h * HEAD_DIM, (h + 1) * HEAD_DIM)
            v = jax.nn.gelu(v_ref[rs, cs].astype(F32))
            vn = _rms(v, gain)
            s = jnp.dot(w, vn.astype(BF16), preferred_element_type=F32) + bias
            u = jax.nn.gelu(u_ref[rs, cs].astype(F32))
            z = jax.nn.silu(z_ref[rs, cs].astype(F32))
            o_ref[rs, cs] = (u * s * z).astype(o_ref.dtype)


def _sgu(proj, sgu_w, sgu_b, sgu_g, width, tm=512):
    b, l, _ = proj.shape
    heads = width // HEAD_DIM

    def col_spec(k):
        return pl.BlockSpec((None, tm, width), lambda bi, i: (bi, i, k))

    kern = functools.partial(_sgu_kernel, heads=heads, chunks=tm // SGU_CHUNK)
    return pl.pallas_call(
        kern,
        grid=(b, l // tm),
        in_specs=[
            col_spec(0), col_spec(1), col_spec(2),
            pl.BlockSpec((heads, SGU_CHUNK, SGU_CHUNK), lambda bi, i: (0, 0, 0)),
            pl.BlockSpec((heads, SGU_CHUNK, 1), lambda bi, i: (0, 0, 0)),
            pl.BlockSpec((1, width), lambda bi, i: (0, 0)),
        ],
        out_specs=pl.BlockSpec((None, tm, width), lambda bi, i: (bi, i, 0)),
        out_shape=jax.ShapeDtypeStruct((b, l, width), BF16),
        compiler_params=_params(("parallel", "parallel")),
        name="sgu",
    )(proj, proj, proj, sgu_w, sgu_b.reshape(heads, SGU_CHUNK, 1), sgu_g.reshape(1, width))


def _attn_kernel(q_ref, k_ref, v_ref, bz_ref, o_ref, acc_scr, car_scr, *, tq, sub, scale):
    qi = pl.program_id(2)
    q = q_ref[...]
    n_sub = tq // sub
    r_i = lax.broadcasted_iota(jnp.int32, (sub, sub), 0)
    c_i = lax.broadcasted_iota(jnp.int32, (sub, sub), 1)
    later_mat = (r_i > c_i).astype(BF16)

    acc_scr[...] = jnp.zeros_like(acc_scr)
    car_scr[...] = jnp.zeros_like(car_scr)

    def block(j, masked):
        ks = pl.multiple_of(j * tq, tq)
        k = k_ref[pl.ds(ks, tq), :]
        v = v_ref[pl.ds(ks, tq), :]
        z = lax.dot_general(q, k, (((1,), (1,)), ((), ())),
                            preferred_element_type=F32) * scale
        soft = jnp.log(1.0 + jnp.exp(-jnp.abs(z)))
        log_beta = jnp.minimum(z, 0.0) - soft
        log_keep = log_beta - z
        if masked:
            mask = (lax.broadcasted_iota(jnp.int32, (tq, tq), 1)
                    < lax.broadcasted_iota(jnp.int32, (tq, tq), 0))
            log_keep = jnp.where(mask, log_keep, 0.0)
        carry = car_scr[...]
        ws = [None] * n_sub
        for i in reversed(range(n_sub)):
            cols = slice(i * sub, (i + 1) * sub)
            lk = log_keep[:, cols]
            hi = lk.astype(BF16)
            lo = (lk - hi.astype(F32)).astype(BF16)
            later = (jnp.dot(hi, later_mat, preferred_element_type=F32)
                     + jnp.dot(lo, later_mat, preferred_element_type=F32))
            ws[i] = jnp.exp(log_beta[:, cols] + later + carry)
            carry = carry + later[:, :1] + lk[:, :1]
        w = jnp.concatenate(ws, axis=1)
        if masked:
            w = jnp.where(mask, w, 0.0)
        acc_scr[...] += jnp.dot(w.astype(BF16), v, preferred_element_type=F32)
        car_scr[...] = carry

    block(qi, True)

    def body(it, _):
        block(qi - 1 - it, False)
        return 0

    lax.fori_loop(0, qi, body, 0)
    gate = jax.nn.silu(bz_ref[...].astype(F32))
    o_ref[...] = (acc_scr[...] * gate).astype(o_ref.dtype)


def _attention(proj, q_col, k_col, v_col, z_col, heads, tq=512, sub=256):
    b, l, _ = proj.shape
    kern = functools.partial(_attn_kernel, tq=tq, sub=sub, scale=1.0 / math.sqrt(HEAD_DIM))
    return pl.pallas_call(
        kern,
        grid=(b, heads, l // tq),
        in_specs=[
            pl.BlockSpec((None, tq, HEAD_DIM), lambda bi, h, i: (bi, i, q_col + h)),
            pl.BlockSpec((None, l, HEAD_DIM), lambda bi, h, i: (bi, 0, k_col + h)),
            pl.BlockSpec((None, l, HEAD_DIM), lambda bi, h, i: (bi, 0, v_col + h)),
            pl.BlockSpec((None, tq, HEAD_DIM), lambda bi, h, i: (bi, i, z_col + h)),
        ],
        out_specs=pl.BlockSpec((None, tq, HEAD_DIM), lambda bi, h, i: (bi, i, h)),
        out_shape=jax.ShapeDtypeStruct((b, l, heads * HEAD_DIM), BF16),
        scratch_shapes=[pltpu.VMEM((tq, HEAD_DIM), F32), pltpu.VMEM((tq, 1), F32)],
        compiler_params=_params(("parallel", "parallel", "parallel")),
        name="stickbreak_attn",
    )(proj, proj, proj, proj)


def _ab_out_kernel(a_ref, b_ref, wa_ref, wb_ref, x_ref, gate_ref, g_ref, o_ref):
    y = (jnp.dot(a_ref[...], wa_ref[...], preferred_element_type=F32)
         + jnp.dot(b_ref[...], wb_ref[...], preferred_element_type=F32))
    o_ref[...] = x_ref[...] + gate_ref[...] * _rms(y, g_ref[...])


def _ab_out(out_a, out_b, w_out, x, gate, g_post, tm=512):
    b, l, d = x.shape
    wa = out_a.shape[-1]
    wb = out_b.shape[-1]
    assert wa == wb
    return pl.pallas_call(
        _ab_out_kernel,
        grid=(b, l // tm),
        in_specs=[
            pl.BlockSpec((None, tm, wa), lambda bi, i: (bi, i, 0)),
            pl.BlockSpec((None, tm, wb), lambda bi, i: (bi, i, 0)),
            pl.BlockSpec((wa, d), lambda bi, i: (0, 0)),
            pl.BlockSpec((wb, d), lambda bi, i: (1, 0)),
            pl.BlockSpec((None, tm, d), lambda bi, i: (bi, i, 0)),
            pl.BlockSpec((None, 1, d), lambda bi, i: (bi, 0, 0)),
            pl.BlockSpec((1, d), lambda bi, i: (0, 0)),
        ],
        out_specs=pl.BlockSpec((None, tm, d), lambda bi, i: (bi, i, 0)),
        out_shape=jax.ShapeDtypeStruct((b, l, d), F32),
        compiler_params=_params(("parallel", "parallel")),
        name="ab_out",
    )(out_a, out_b, w_out, w_out, x, gate.reshape(b, 1, d), g_post.reshape(1, d))


def _ssm_prep_kernel(lr_ref, li_ref, ldt_ref, br_ref, bi_ref, cr_ref, ci_ref,
                     k_ref, bsr_ref, bsi_ref, csr_ref, csi_ref, atr_ref, ati_ref):
    lr = lr_ref[...]
    li = li_ref[...]
    dt = jnp.exp(ldt_ref[...])
    mag = jnp.exp(lr * dt)
    a_re = mag * jnp.cos(li * dt)
    a_im = mag * jnp.sin(li * dt)
    den = lr * lr + li * li
    nr = a_re - 1.0
    coef_re = (nr * lr + a_im * li) / den
    coef_im = (a_im * lr - nr * li) / den
    b_re = br_ref[...]
    b_im = bi_ref[...]
    bb_re = coef_re * b_re - coef_im * b_im
    bb_im = coef_re * b_im + coef_im * b_re
    c_re = cr_ref[...]
    c_im = ci_ref[...]
    rows = lr.shape[0]
    nt = (((1,), (1,)), ((), ()))
    hp = lax.Precision.HIGHEST
    p_re = jnp.ones_like(lr)
    p_im = jnp.zeros_like(lr)
    for d in range(SSM_T + 1):
        cd_re = c_re * p_re - c_im * p_im
        cd_im = c_re * p_im + c_im * p_re
        if d >= 1:
            csr_ref[d - 1] = cd_re
            csi_ref[d - 1] = -cd_im
        if d == SSM_T:
            atr_ref[...] = p_re
            ati_ref[...] = p_im
            break
        s = SSM_T - 1 - d
        bsr_ref[s] = bb_re * p_re - bb_im * p_im
        bsi_ref[s] = bb_re * p_im + bb_im * p_re
        for blk in range(rows // LANES):
            rs = slice(blk * LANES, (blk + 1) * LANES)
            k_ref[d, rs, :] = (
                lax.dot_general(cd_re[rs], bb_re[rs], nt, precision=hp,
                                preferred_element_type=F32)
                - lax.dot_general(cd_im[rs], bb_im[rs], nt, precision=hp,
                                  preferred_element_type=F32))
        p_re, p_im = p_re * a_re - p_im * a_im, p_re * a_im + p_im * a_re


def _ssm_prep(lam_re, lam_im, log_dt, b_re, b_im, c_re, c_im):
    g, p = lam_re.shape
    rows = g * SSM_GROUP
    rep = lambda a: jnp.repeat(a, SSM_GROUP, axis=0)
    b_t = lambda a: jnp.transpose(a, (0, 2, 1)).reshape(rows, p)
    ldt = jnp.broadcast_to(rep(log_dt.reshape(g, 1)), (rows, p))
    full = lambda: pl.BlockSpec((rows, p), lambda: (0, 0))
    stack = lambda n: pl.BlockSpec((SSM_T, rows, n), lambda: (0, 0, 0))
    return pl.pallas_call(
        _ssm_prep_kernel,
        in_specs=[full() for _ in range(7)],
        out_specs=[stack(LANES), stack(p), stack(p), stack(p), stack(p), full(), full()],
        out_shape=[
            jax.ShapeDtypeStruct((SSM_T, rows, LANES), F32),
            jax.ShapeDtypeStruct((SSM_T, rows, p), F32),
            jax.ShapeDtypeStruct((SSM_T, rows, p), F32),
            jax.ShapeDtypeStruct((SSM_T, rows, p), F32),
            jax.ShapeDtypeStruct((SSM_T, rows, p), F32),
            jax.ShapeDtypeStruct((rows, p), F32),
            jax.ShapeDtypeStruct((rows, p), F32),
        ],
        compiler_params=pltpu.CompilerParams(vmem_limit_bytes=VMEM_LIMIT),
        name="ssm_prep",
    )(rep(lam_re), rep(lam_im), ldt, b_t(b_re), b_t(b_im),
      c_re.reshape(rows, p), c_im.reshape(rows, p))


def _ssm_assemble(kf, bsr, bsi, csr, csi, atr, ati, d_skip):
    t = SSM_T
    gs = GROUPS_PER_SLAB
    c = SSM_GROUP
    p = SSM_STATE
    rows = kf.shape[1]
    slabs = rows // LANES
    eye = jnp.eye(gs, dtype=bool)
    k6 = kf.reshape(t, slabs, gs, c, gs, c)
    idx = jnp.arange(gs)
    kd = k6[:, :, idx, :, idx, :]
    s_i = jnp.arange(t)[:, None]
    t_i = jnp.arange(t)[None, :]
    lag = jnp.clip(t_i - s_i, 0, t - 1)
    kg = kd[:, lag]
    kg = jnp.where((t_i >= s_i)[None, :, :, None, None, None], kg, 0.0)
    kg = jnp.transpose(kg, (3, 1, 0, 5, 2, 4))
    toep = jnp.where(eye[None, None, :, None, None, :, None],
                     kg[:, :, :, :, :, None, :], 0.0)
    toep = toep.reshape(slabs, t * LANES, t * LANES)
    bs = jnp.stack([bsr, bsi], axis=-2)
    bs = bs.reshape(t, slabs, gs, c, 2, p)
    bs = jnp.transpose(bs, (1, 0, 2, 3, 4, 5))
    bs = jnp.where(eye[None, None, :, None, None, :, None],
                   bs[:, :, :, :, :, None, :], 0.0)
    bs = bs.reshape(slabs, t * LANES, 2 * gs * p)
    w_in = jnp.concatenate([toep, bs], axis=-1).astype(BF16)
    cs = jnp.stack([csr, csi], axis=0)
    cs = cs.reshape(2, t, slabs, gs, c, p)
    cs = jnp.transpose(cs, (2, 0, 3, 5, 1, 4))
    cs = jnp.where(eye[None, None, :, None, None, :, None],
                   cs[:, :, :, :, :, None, :], 0.0)
    cs = cs.reshape(slabs, 2 * gs * p, t * LANES).astype(BF16)
    a_t = jnp.concatenate([atr[::c].reshape(slabs, 1, gs * p),
                           ati[::c].reshape(slabs, 1, gs * p)], axis=-1)
    dsk = jnp.tile(d_skip.reshape(slabs, 1, LANES), (1, 1, t))
    return w_in, cs, a_t, dsk


def _ssm_kernel(u_ref, w_ref, cs_ref, at_ref, dsk_ref, y_ref, s_scr, hp_scr, h_scr,
                *, batch, n_chunks):
    ns = SLAB_STATE
    out_w = SSM_T * LANES

    @pl.when(pl.program_id(1) == 0)
    def _():
        h_scr[...] = jnp.zeros_like(h_scr)

    u = jnp.concatenate([u_ref[:, s, :] for s in range(SSM_T)], axis=-1)
    r = jnp.dot(u.astype(BF16), w_ref[...], preferred_element_type=F32)
    s_scr[...] = r[:, out_w:]
    a_re = jnp.broadcast_to(at_ref[:, :ns], (batch, ns))
    a_im = jnp.broadcast_to(at_ref[:, ns:], (batch, ns))

    sub = 8
    per_tile = sub // batch

    def step(k, carry):
        h_re, h_im = carry
        r0 = pl.multiple_of(k * sub, sub)
        s_re = s_scr[pl.ds(r0, sub), :ns]
        s_im = s_scr[pl.ds(r0, sub), ns:]
        prev_re, prev_im = [], []
        for c in range(per_tile):
            rows = slice(c * batch, (c + 1) * batch)
            prev_re.append(h_re)
            prev_im.append(h_im)
            h_re, h_im = (a_re * h_re - a_im * h_im + s_re[rows],
                          a_re * h_im + a_im * h_re + s_im[rows])
        hp_scr[pl.ds(r0, sub), :ns] = jnp.concatenate(prev_re, axis=0)
        hp_scr[pl.ds(r0, sub), ns:] = jnp.concatenate(prev_im, axis=0)
        return h_re, h_im

    h_re, h_im = lax.fori_loop(0, n_chunks // per_tile, step,
                               (h_scr[:, :ns], h_scr[:, ns:]))
    h_scr[:, :ns] = h_re
    h_scr[:, ns:] = h_im
    y = (r[:, :out_w]
         + jnp.dot(hp_scr[...].astype(BF16), cs_ref[...], preferred_element_type=F32)
         + dsk_ref[...] * u)
    for t in range(SSM_T):
        y_ref[:, t, :] = y[:, t * LANES:(t + 1) * LANES]


def _ssm(u5, w_in, cs, a_t, dsk, n_rows=512):
    lc, b, slabs, t, _ = u5.shape
    rows = lc * b
    width = t * LANES
    u4 = u5.reshape(rows, slabs, t, LANES)
    n_rows = min(n_rows, rows)
    kern = functools.partial(_ssm_kernel, batch=b, n_chunks=n_rows // b)
    y4 = pl.pallas_call(
        kern,
        grid=(slabs, rows // n_rows),
        in_specs=[
            pl.BlockSpec((n_rows, None, t, LANES), lambda j, i: (i, j, 0, 0)),
            pl.BlockSpec((None, width, width + 2 * SLAB_STATE), lambda j, i: (j, 0, 0)),
            pl.BlockSpec((None, 2 * SLAB_STATE, width), lambda j, i: (j, 0, 0)),
            pl.BlockSpec((None, 1, 2 * SLAB_STATE), lambda j, i: (j, 0, 0)),
            pl.BlockSpec((None, 1, width), lambda j, i: (j, 0, 0)),
        ],
        out_specs=pl.BlockSpec((n_rows, None, t, LANES), lambda j, i: (i, j, 0, 0)),
        out_shape=jax.ShapeDtypeStruct((rows, slabs, t, LANES), F32),
        scratch_shapes=[
            pltpu.VMEM((n_rows, 2 * SLAB_STATE), F32),
            pltpu.VMEM((n_rows, 2 * SLAB_STATE), F32),
            pltpu.VMEM((b, 2 * SLAB_STATE), F32),
        ],
        compiler_params=_params(("parallel", "arbitrary")),
        name="ssm_scan",
    )(u4, w_in, cs, a_t, dsk)
    return y4.reshape(lc, b, slabs, t, LANES)


def _glu_out_kernel(y_ref, z_ref, wg_ref, bg_ref, wo_ref, x_ref, gate_ref, g_ref, o_ref):
    slabs = y_ref.shape[1]
    tm = z_ref.shape[0]
    y = jnp.concatenate([y_ref[:, j].reshape(tm, LANES) for j in range(slabs)], axis=-1)
    g = jax.nn.gelu(y)
    t = jnp.dot(g.astype(BF16), wg_ref[...], preferred_element_type=F32) + bg_ref[...]
    y2 = g * jax.nn.sigmoid(t) * jax.nn.silu(z_ref[...].astype(F32))
    o = jnp.dot(y2.astype(BF16), wo_ref[...], preferred_element_type=F32)
    o_ref[...] = x_ref[...] + gate_ref[...] * _rms(o, g_ref[...])


def _glu_out(y5, z, w_glu, b_glu, w_out, x, gate, g_post, tm=512):
    b, l, d = x.shape
    _, _, slabs, t, _ = y5.shape
    width = slabs * LANES
    return pl.pallas_call(
        _glu_out_kernel,
        grid=(b, l // tm),
        in_specs=[
            pl.BlockSpec((tm // t, None, slabs, t, LANES), lambda bi, i: (i, bi, 0, 0, 0)),
            pl.BlockSpec((None, tm, width), lambda bi, i: (bi, i, 0)),
            pl.BlockSpec((width, width), lambda bi, i: (0, 0)),
            pl.BlockSpec((1, width), lambda bi, i: (0, 0)),
            pl.BlockSpec((width, d), lambda bi, i: (0, 0)),
            pl.BlockSpec((None, tm, d), lambda bi, i: (bi, i, 0)),
            pl.BlockSpec((None, 1, d), lambda bi, i: (bi, 0, 0)),
            pl.BlockSpec((1, d), lambda bi, i: (0, 0)),
        ],
        out_specs=pl.BlockSpec((None, tm, d), lambda bi, i: (bi, i, 0)),
        out_shape=jax.ShapeDtypeStruct((b, l, d), F32),
        compiler_params=_params(("parallel", "parallel")),
        name="glu_out",
    )(y5, z, w_glu, b_glu.reshape(1, width), w_out, x, gate.reshape(b, 1, d),
      g_post.reshape(1, d))


def _ab_layer(x, mod, g_pre, g_post, w_in, w_out, sgu_g, sgu_w, sgu_b):
    d = x.shape[-1]
    shift, scale, gate = mod[:, :d], mod[:, d:2 * d], mod[:, 2 * d:]
    wa = d // 2
    heads = wa // HEAD_DIM
    proj = _inproj(x, g_pre, scale, shift, w_in.astype(BF16))
    out_a = _sgu(proj, sgu_w, sgu_b, sgu_g, wa)
    blk = lambda cols: cols // HEAD_DIM
    out_b = _attention(proj, blk(3 * wa), blk(4 * wa), blk(5 * wa), blk(6 * wa), heads)
    return _ab_out(out_a, out_b, w_out.astype(BF16), x, gate, g_post)


def _ssm_layer(x, mod, g_pre, g_post, w_in, w_out, lam_re, lam_im, b_re, b_im, c_re, c_im,
               d_skip, log_dt, w_glu, b_glu):
    d = x.shape[-1]
    shift, scale, gate = mod[:, :d], mod[:, d:2 * d], mod[:, 2 * d:]
    u5, z = _inproj_ssm(x, g_pre, scale, shift, w_in.astype(BF16))
    prep = _ssm_prep(lam_re, lam_im, log_dt, b_re, b_im, c_re, c_im)
    y5 = _ssm(u5, *_ssm_assemble(*prep, d_skip))
    return _glu_out(y5, z, w_glu.astype(BF16), b_glu, w_out.astype(BF16), x, gate, g_post)


def kernel(x, c, ln_pre_g, ln_post_g, w_mod, b_mod, w_in_ab, w_out_ab, sgu_norm_g, sgu_w, sgu_b,
           w_in_ssm, w_out_ssm, lam_re, lam_im, b_re, b_im, c_re, c_im, d_skip, log_dt,
           w_glu, b_glu):
    depth = w_mod.shape[0]
    batch = x.shape[0]
    pad = (-batch) % 8
    c_pad = jnp.pad(c, ((0, pad), (0, 0)))
    mod = _adaln_mod(c_pad, w_mod, b_mod)[:, :batch]
    for layer in range(depth):
        i = layer // 2
        if layer % 2 == 0:
            x = _ab_layer(x, mod[layer], ln_pre_g[layer], ln_post_g[layer], w_in_ab[i],
                          w_out_ab[i], sgu_norm_g[i], sgu_w[i], sgu_b[i])
        else:
            x = _ssm_layer(x, mod[layer], ln_pre_g[layer], ln_post_g[layer], w_in_ssm[i],
                           w_out_ssm[i], lam_re[i], lam_im[i], b_re[i], b_im[i], c_re[i],
                           c_im[i], d_skip[i], log_dt[i], w_glu[i], b_glu[i])
    return x
```

```python
import functools
import math

import jax
import jax.numpy as jnp
from jax import lax
from jax.experimental import pallas as pl
from jax.experimental.pallas import tpu as pltpu

F32 = jnp.float32
BF16 = jnp.bfloat16

EPS = 1e-6
LANES = 128
HEAD_DIM = 128
SGU_CHUNK = 128
SSM_GROUP = 16
SSM_STATE = 64
SSM_T = 8
GROUPS_PER_SLAB = LANES // SSM_GROUP
SLAB_STATE = GROUPS_PER_SLAB * SSM_STATE
VMEM_LIMIT = 56 * 1024 * 1024


def _params(sem, vmem=VMEM_LIMIT):
    return pltpu.CompilerParams(dimension_semantics=sem, vmem_limit_bytes=vmem)


def _rms(y, g):
    ms = jnp.mean(y * y, axis=-1, keepdims=True)
    return y * lax.rsqrt(ms + EPS) * g


def _mod_kernel(c_ref, w_ref, b_ref, o_ref):
    cond = jax.nn.silu(c_ref[...])
    o_ref[...] = jnp.dot(cond.astype(BF16), w_ref[...].astype(BF16),
                         preferred_element_type=F32) + b_ref[...]


def _adaln_mod(c_pad, w_mod, b_mod, tn=512):
    depth, d, n = w_mod.shape
    rows = c_pad.shape[0]
    return pl.pallas_call(
        _mod_kernel,
        grid=(depth, n // tn),
        in_specs=[
            pl.BlockSpec((rows, d), lambda l, j: (0, 0)),
            pl.BlockSpec((None, d, tn), lambda l, j: (l, 0, j)),
            pl.BlockSpec((None, 1, tn), lambda l, j: (l, 0, j)),
        ],
        out_specs=pl.BlockSpec((None, rows, tn), lambda l, j: (l, 0, j)),
        out_shape=jax.ShapeDtypeStruct((depth, rows, n), F32),
        compiler_params=_params(("parallel", "parallel")),
        name="adaln_mod",
    )(c_pad, w_mod, b_mod.reshape(depth, 1, n))


def _prenorm(x_ref, g_ref, sc_ref, sh_ref):
    y = _rms(x_ref[...], g_ref[...])
    return (y * (1.0 + sc_ref[...]) + sh_ref[...]).astype(BF16)


def _inproj_kernel(x_ref, g_ref, sc_ref, sh_ref, w_ref, o_ref, h_scr):
    @pl.when(pl.program_id(2) == 0)
    def _():
        h_scr[...] = _prenorm(x_ref, g_ref, sc_ref, sh_ref)

    o_ref[...] = jnp.dot(h_scr[...], w_ref[...],
                         preferred_element_type=F32).astype(o_ref.dtype)


def _inproj(x, g, scale, shift, w, tm=512, tn=1024):
    b, l, d = x.shape
    n = w.shape[1]
    tn = math.gcd(n, tn)
    return pl.pallas_call(
        _inproj_kernel,
        grid=(b, l // tm, n // tn),
        in_specs=[
            pl.BlockSpec((None, tm, d), lambda bi, i, j: (bi, i, 0)),
            pl.BlockSpec((1, d), lambda bi, i, j: (0, 0)),
            pl.BlockSpec((None, 1, d), lambda bi, i, j: (bi, 0, 0)),
            pl.BlockSpec((None, 1, d), lambda bi, i, j: (bi, 0, 0)),
            pl.BlockSpec((d, tn), lambda bi, i, j: (0, j)),
        ],
        out_specs=pl.BlockSpec((None, tm, tn), lambda bi, i, j: (bi, i, j)),
        out_shape=jax.ShapeDtypeStruct((b, l, n), BF16),
        scratch_shapes=[pltpu.VMEM((tm, d), BF16)],
        compiler_params=_params(("parallel", "parallel", "arbitrary")),
        name="inproj_ab",
    )(x, g.reshape(1, d), scale.reshape(b, 1, d), shift.reshape(b, 1, d), w)


def _inproj_ssm_kernel(x_ref, g_ref, sc_ref, sh_ref, w_ref, u_ref, z_ref, *, width):
    h = _prenorm(x_ref, g_ref, sc_ref, sh_ref)
    acc = jnp.dot(h, w_ref[...], preferred_element_type=F32)
    tm = acc.shape[0]
    for j in range(width // LANES):
        u_ref[:, j] = acc[:, j * LANES:(j + 1) * LANES].reshape(tm // SSM_T, SSM_T, LANES)
    z_ref[...] = acc[:, width:].astype(z_ref.dtype)


def _inproj_ssm(x, g, scale, shift, w, tm=512):
    b, l, d = x.shape
    width = w.shape[1] // 2
    slabs = width // LANES
    kern = functools.partial(_inproj_ssm_kernel, width=width)
    return pl.pallas_call(
        kern,
        grid=(b, l // tm),
        in_specs=[
            pl.BlockSpec((None, tm, d), lambda bi, i: (bi, i, 0)),
            pl.BlockSpec((1, d), lambda bi, i: (0, 0)),
            pl.BlockSpec((None, 1, d), lambda bi, i: (bi, 0, 0)),
            pl.BlockSpec((None, 1, d), lambda bi, i: (bi, 0, 0)),
            pl.BlockSpec((d, 2 * width), lambda bi, i: (0, 0)),
        ],
        out_specs=[
            pl.BlockSpec((tm // SSM_T, None, slabs, SSM_T, LANES),
                         lambda bi, i: (i, bi, 0, 0, 0)),
            pl.BlockSpec((None, tm, width), lambda bi, i: (bi, i, 0)),
        ],
        out_shape=[
            jax.ShapeDtypeStruct((l // SSM_T, b, slabs, SSM_T, LANES), F32),
            jax.ShapeDtypeStruct((b, l, width), BF16),
        ],
        compiler_params=_params(("parallel", "parallel")),
        name="inproj_ssm",
    )(x, g.reshape(1, d), scale.reshape(b, 1, d), shift.reshape(b, 1, d), w)


def _sgu_kernel(u_ref, v_ref, z_ref, w_ref, b_ref, g_ref, o_ref, *, heads, chunks):
    row = lax.broadcasted_iota(jnp.int32, (SGU_CHUNK, SGU_CHUNK), 0)
    col = lax.broadcasted_iota(jnp.int32, (SGU_CHUNK, SGU_CHUNK), 1)
    causal = col <= row
    for h in range(heads):
        w = jnp.where(causal, w_ref[h], 0.0).astype(BF16)
        bias = b_ref[h]
        gain = g_ref[:, h * HEAD_DIM:(h + 1) * HEAD_DIM]
        for c in range(chunks):
            rs = slice(c * SGU_CHUNK, (c + 1) * SGU_CHUNK)
            cs = slice(h * HEAD_DIM, (h + 1) * HEAD_DIM)
            v = jax.nn.gelu(v_ref[rs, cs].astype(F32))
            vn = _rms(v, gain)
            s = jnp.dot(w, vn.astype(BF16), preferred_element_type=F32) + bias
            u = jax.nn.gelu(u_ref[rs, cs].astype(F32))
            z = jax.nn.silu(z_ref[rs, cs].astype(F32))
            o_ref[rs, cs] = (u * s * z).astype(o_ref.dtype)


def _sgu(proj, sgu_w, sgu_b, sgu_g, width, tm=512):
    b, l, _ = proj.shape
    heads = width // HEAD_DIM

    def col_spec(k):
        return pl.BlockSpec((None, tm, width), lambda bi, i: (bi, i, k))

    kern = functools.partial(_sgu_kernel, heads=heads, chunks=tm // SGU_CHUNK)
    return pl.pallas_call(
        kern,
        grid=(b, l // tm),
        in_specs=[
            col_spec(0), col_spec(1), col_spec(2),
            pl.BlockSpec((heads, SGU_CHUNK, SGU_CHUNK), lambda bi, i: (0, 0, 0)),
            pl.BlockSpec((heads, SGU_CHUNK, 1), lambda bi, i: (0, 0, 0)),
            pl.BlockSpec((1, width), lambda bi, i: (0, 0)),
        ],
        out_specs=pl.BlockSpec((None, tm, width), lambda bi, i: (bi, i, 0)),
        out_shape=jax.ShapeDtypeStruct((b, l, width), BF16),
        compiler_params=_params(("parallel", "parallel")),
        name="sgu",
    )(proj, proj, proj, sgu_w, sgu_b.reshape(heads, SGU_CHUNK, 1), sgu_g.reshape(1, width))


def _attn_kernel(q_ref, k_ref, v_ref, bz_ref, o_ref, acc_scr, car_scr, *, tq, sub, scale):
    qi = pl.program_id(2)
    q = q_ref[...]
    n_sub = tq // sub
    r_i = lax.broadcasted_iota(jnp.int32, (sub, sub), 0)
    c_i = lax.broadcasted_iota(jnp.int32, (sub, sub), 1)
    later_mat = (r_i > c_i).astype(BF16)

    acc_scr[...] = jnp.zeros_like(acc_scr)
    car_scr[...] = jnp.zeros_like(car_scr)

    def block(j, masked):
        ks = pl.multiple_of(j * tq, tq)
        k = k_ref[pl.ds(ks, tq), :]
        v = v_ref[pl.ds(ks, tq), :]
        z = lax.dot_general(q, k, (((1,), (1,)), ((), ())),
                            preferred_element_type=F32) * scale
        soft = jnp.log(1.0 + jnp.exp(-jnp.abs(z)))
        log_beta = jnp.minimum(z, 0.0) - soft
        log_keep = log_beta - z
        if masked:
            mask = (lax.broadcasted_iota(jnp.int32, (tq, tq), 1)
                    < lax.broadcasted_iota(jnp.int32, (tq, tq), 0))
            log_keep = jnp.where(mask, log_keep, 0.0)
        carry = car_scr[...]
        ws = [None] * n_sub
        for i in reversed(range(n_sub)):
            cols = slice(i * sub, (i + 1) * sub)
            lk = log_keep[:, cols]
            hi = lk.astype(BF16)
            lo = (lk - hi.astype(F32)).astype(BF16)
            later = (jnp.dot(hi, later_mat, preferred_element_type=F32)
                     + jnp.dot(lo, later_mat, preferred_element_type=F32))
            ws[i] = jnp.exp(log_beta[:, cols] + later + carry)
            carry = carry + later[:, :1] + lk[:, :1]
        w = jnp.concatenate(ws, axis=1)
        if masked:
            w = jnp.where(mask, w, 0.0)
        acc_scr[...] += jnp.dot(w.astype(BF16), v, preferred_element_type=F32)
        car_scr[...] = carry

    block(qi, True)

    def body(it, _):
        block(qi - 1 - it, False)
        return 0

    lax.fori_loop(0, qi, body, 0)
    gate = jax.nn.silu(bz_ref[...].astype(F32))
    o_ref[...] = (acc_scr[...] * gate).astype(o_ref.dtype)


def _attention(proj, q_col, k_col, v_col, z_col, heads, tq=512, sub=256):
    b, l, _ = proj.shape
    kern = functools.partial(_attn_kernel, tq=tq, sub=sub, scale=1.0 / math.sqrt(HEAD_DIM))
    return pl.pallas_call(
        kern,
        grid=(b, heads, l // tq),
        in_specs=[
            pl.BlockSpec((None, tq, HEAD_DIM), lambda bi, h, i: (bi, i, q_col + h)),
            pl.BlockSpec((None, l, HEAD_DIM), lambda bi, h, i: (bi, 0, k_col + h)),
            pl.BlockSpec((None, l, HEAD_DIM), lambda bi, h, i: (bi, 0, v_col + h)),
            pl.BlockSpec((None, tq, HEAD_DIM), lambda bi, h, i: (bi, i, z_col + h)),
        ],
        out_specs=pl.BlockSpec((None, tq, HEAD_DIM), lambda bi, h, i: (bi, i, h)),
        out_shape=jax.ShapeDtypeStruct((b, l, heads * HEAD_DIM), BF16),
        scratch_shapes=[pltpu.VMEM((tq, HEAD_DIM), F32), pltpu.VMEM((tq, 1), F32)],
        compiler_params=_params(("parallel", "parallel", "parallel")),
        name="stickbreak_attn",
    )(proj, proj, proj, proj)


def _ab_out_kernel(a_ref, b_ref, wa_ref, wb_ref, x_ref, gate_ref, g_ref, o_ref):
    y = (jnp.dot(a_ref[...], wa_ref[...], preferred_element_type=F32)
         + jnp.dot(b_ref[...], wb_ref[...], preferred_element_type=F32))
    o_ref[...] = x_ref[...] + gate_ref[...] * _rms(y, g_ref[...])


def _ab_out(out_a, out_b, w_out, x, gate, g_post, tm=512):
    b, l, d = x.shape
    wa = out_a.shape[-1]
    wb = out_b.shape[-1]
    assert wa == wb
    return pl.pallas_call(
        _ab_out_kernel,
        grid=(b, l // tm),
        in_specs=[
            pl.BlockSpec((None, tm, wa), lambda bi, i: (bi, i, 0)),
            pl.BlockSpec((None, tm, wb), lambda bi, i: (bi, i, 0)),
            pl.BlockSpec((wa, d), lambda bi, i: (0, 0)),
            pl.BlockSpec((wb, d), lambda bi, i: (1, 0)),
            pl.BlockSpec((None, tm, d), lambda bi, i: (bi, i, 0)),
            pl.BlockSpec((None, 1, d), lambda bi, i: (bi, 0, 0)),
            pl.BlockSpec((1, d), lambda bi, i: (0, 0)),
        ],
        out_specs=pl.BlockSpec((None, tm, d), lambda bi, i: (bi, i, 0)),
        out_shape=jax.ShapeDtypeStruct((b, l, d), F32),
        compiler_params=_params(("parallel", "parallel")),
        name="ab_out",
    )(out_a, out_b, w_out, w_out, x, gate.reshape(b, 1, d), g_post.reshape(1, d))


def _discretise(lr, li, ldt):
    dt = jnp.exp(ldt)
    mag = jnp.exp(lr * dt)
    return mag * jnp.cos(li * dt), mag * jnp.sin(li * dt)


def _cmul(x_re, x_im, y_re, y_im):
    return x_re * y_re - x_im * y_im, x_re * y_im + x_im * y_re


def _ssm_prep_kernel(lrc_ref, lic_ref, ldc_ref, br_ref, bi_ref, cr_ref, ci_ref,
                     lrp_ref, lip_ref, ldp_ref, ctr_ref, cti_ref,
                     lrl_ref, lil_ref, ldl_ref,
                     w_ref, cs_ref, at_ref):
    t_steps = SSM_T
    ns = SLAB_STATE
    out_w = t_steps * LANES
    nt = (((1,), (1,)), ((), ()))
    hp = lax.Precision.HIGHEST
    lr = lrc_ref[...]
    li = lic_ref[...]
    a_re, a_im = _discretise(lr, li, ldc_ref[...])
    den = lr * lr + li * li
    nr = a_re - 1.0
    coef_re = (nr * lr + a_im * li) / den
    coef_im = (a_im * lr - nr * li) / den
    bb_re, bb_im = _cmul(coef_re, coef_im, br_ref[...], bi_ref[...])
    row_g = lax.broadcasted_iota(jnp.int32, lr.shape, 0) // SSM_GROUP
    col_g = lax.broadcasted_iota(jnp.int32, lr.shape, 1) // SSM_STATE
    same_state = row_g == col_g
    row_k = lax.broadcasted_iota(jnp.int32, (LANES, LANES), 0) // SSM_GROUP
    col_k = lax.broadcasted_iota(jnp.int32, (LANES, LANES), 1) // SSM_GROUP
    same_lag = row_k == col_k
    c_re = cr_ref[...]
    c_im = ci_ref[...]
    a64_re = a_re[:, :SSM_STATE]
    a64_im = a_im[:, :SSM_STATE]
    bb64_re = bb_re[:, :SSM_STATE]
    bb64_im = bb_im[:, :SSM_STATE]
    p_re = jnp.ones_like(lr)
    p_im = jnp.zeros_like(lr)
    q_re = jnp.ones_like(a64_re)
    q_im = jnp.zeros_like(a64_re)
    lag_blocks = []
    for d in range(t_steps):
        s = t_steps - 1 - d
        bs_re, bs_im = _cmul(bb_re, bb_im, p_re, p_im)
        w_ref[s, :, out_w:out_w + ns] = jnp.where(same_state, bs_re, 0.0).astype(w_ref.dtype)
        w_ref[s, :, out_w + ns:] = jnp.where(same_state, bs_im, 0.0).astype(w_ref.dtype)
        cd_re, cd_im = _cmul(c_re, c_im, q_re, q_im)
        kd = (lax.dot_general(bb64_re, cd_re, nt, precision=hp, preferred_element_type=F32)
              - lax.dot_general(bb64_im, cd_im, nt, precision=hp, preferred_element_type=F32))
        lag_blocks.append(jnp.where(same_lag, kd, 0.0).astype(w_ref.dtype))
        p_re, p_im = _cmul(p_re, p_im, a_re, a_im)
        q_re, q_im = _cmul(q_re, q_im, a64_re, a64_im)
    zero_block = jnp.zeros((LANES, LANES), w_ref.dtype)
    for s in range(t_steps):
        for t in range(t_steps):
            w_ref[s, :, t * LANES:(t + 1) * LANES] = lag_blocks[t - s] if t >= s else zero_block
    ap_re, ap_im = _discretise(lrp_ref[...], lip_ref[...], ldp_ref[...])
    row_p = lax.broadcasted_iota(jnp.int32, ap_re.shape, 0) // SSM_STATE
    col_p = lax.broadcasted_iota(jnp.int32, ap_re.shape, 1) // SSM_GROUP
    same_out = row_p == col_p
    ct_re = ctr_ref[...]
    ct_im = cti_ref[...]
    r_re, r_im = ap_re, ap_im
    for t in range(t_steps):
        cd_re, cd_im = _cmul(ct_re, ct_im, r_re, r_im)
        cols = slice(t * LANES, (t + 1) * LANES)
        cs_ref[:ns, cols] = jnp.where(same_out, cd_re, 0.0).astype(cs_ref.dtype)
        cs_ref[ns:, cols] = jnp.where(same_out, -cd_im, 0.0).astype(cs_ref.dtype)
        r_re, r_im = _cmul(r_re, r_im, ap_re, ap_im)
    al_re, al_im = _discretise(lrl_ref[...], lil_ref[...], ldl_ref[...])
    t_re, t_im = al_re, al_im
    for _ in range(t_steps - 1):
        t_re, t_im = _cmul(t_re, t_im, al_re, al_im)
    at_ref[:, :ns] = t_re
    at_ref[:, ns:] = t_im


def _ssm_prep(lam_re, lam_im, log_dt, b_re, b_im, c_re, c_im, d_skip):
    g, p = lam_re.shape
    gs = GROUPS_PER_SLAB
    rows = g * SSM_GROUP
    slabs = rows // LANES
    ns = SLAB_STATE
    width = SSM_T * LANES
    ldt = jnp.broadcast_to(log_dt.reshape(g, 1), (g, p))
    chan = lambda a: jnp.tile(jnp.repeat(a, SSM_GROUP, axis=0), (1, gs))
    b_t = lambda a: jnp.tile(jnp.transpose(a, (0, 2, 1)).reshape(rows, p), (1, gs))
    stat = lambda a: jnp.broadcast_to(a.reshape(g * p, 1), (g * p, LANES))
    c_t = lambda a: jnp.tile(jnp.transpose(a, (0, 2, 1)).reshape(g * p, SSM_GROUP), (1, gs))
    lane = lambda a: a.reshape(slabs, 1, ns)
    chan_spec = pl.BlockSpec((LANES, ns), lambda j: (j, 0))
    stat_spec = pl.BlockSpec((ns, LANES), lambda j: (j, 0))
    lane_spec = pl.BlockSpec((None, 1, ns), lambda j: (j, 0, 0))
    w_in, cs, a_t = pl.pallas_call(
        _ssm_prep_kernel,
        grid=(slabs,),
        in_specs=[chan_spec] * 5 + [pl.BlockSpec((LANES, p), lambda j: (j, 0))] * 2
                 + [stat_spec] * 5 + [lane_spec] * 3,
        out_specs=[
            pl.BlockSpec((SSM_T, LANES, width + 2 * ns), lambda j: (0, j, 0)),
            pl.BlockSpec((None, 2 * ns, width), lambda j: (j, 0, 0)),
            pl.BlockSpec((None, 1, 2 * ns), lambda j: (j, 0, 0)),
        ],
        out_shape=[
            jax.ShapeDtypeStruct((SSM_T, rows, width + 2 * ns), BF16),
            jax.ShapeDtypeStruct((slabs, 2 * ns, width), BF16),
            jax.ShapeDtypeStruct((slabs, 1, 2 * ns), F32),
        ],
        compiler_params=_params(("parallel",)),
        name="ssm_prep",
    )(chan(lam_re), chan(lam_im), chan(ldt), b_t(b_re), b_t(b_im),
      c_re.reshape(rows, p), c_im.reshape(rows, p),
      stat(lam_re), stat(lam_im), stat(ldt), c_t(c_re), c_t(c_im),
      lane(lam_re), lane(lam_im), lane(ldt))
    dsk = jnp.tile(d_skip.reshape(slabs, 1, LANES), (1, 1, SSM_T))
    return w_in, cs, a_t, dsk


def _ssm_kernel(u_ref, w_ref, cs_ref, at_ref, dsk_ref, y_ref, s_scr, hp_scr, h_scr,
                *, batch, n_chunks):
    ns = SLAB_STATE
    out_w = SSM_T * LANES

    @pl.when(pl.program_id(1) == 0)
    def _():
        h_scr[...] = jnp.zeros_like(h_scr)

    u = jnp.concatenate([u_ref[:, s, :] for s in range(SSM_T)], axis=-1)
    w = w_ref[...].reshape(out_w, out_w + 2 * ns)
    r = jnp.dot(u.astype(BF16), w, preferred_element_type=F32)
    s_scr[...] = r[:, out_w:]
    a_re = jnp.broadcast_to(at_ref[:, :ns], (batch, ns))
    a_im = jnp.broadcast_to(at_ref[:, ns:], (batch, ns))

    sub = 8
    per_tile = sub // batch

    def step(k, carry):
        h_re, h_im = carry
        r0 = pl.multiple_of(k * sub, sub)
        s_re = s_scr[pl.ds(r0, sub), :ns]
        s_im = s_scr[pl.ds(r0, sub), ns:]
        prev_re, prev_im = [], []
        for c in range(per_tile):
            rows = slice(c * batch, (c + 1) * batch)
            prev_re.append(h_re)
            prev_im.append(h_im)
            h_re, h_im = (a_re * h_re - a_im * h_im + s_re[rows],
                          a_re * h_im + a_im * h_re + s_im[rows])
        hp_scr[pl.ds(r0, sub), :ns] = jnp.concatenate(prev_re, axis=0)
        hp_scr[pl.ds(r0, sub), ns:] = jnp.concatenate(prev_im, axis=0)
        return h_re, h_im

    h_re, h_im = lax.fori_loop(0, n_chunks // per_tile, step,
                               (h_scr[:, :ns], h_scr[:, ns:]))
    h_scr[:, :ns] = h_re
    h_scr[:, ns:] = h_im
    y = (r[:, :out_w]
         + jnp.dot(hp_scr[...].astype(BF16), cs_ref[...], preferred_element_type=F32)
         + dsk_ref[...] * u)
    for t in range(SSM_T):
        y_ref[:, t, :] = y[:, t * LANES:(t + 1) * LANES]


def _ssm(u5, w_in, cs, a_t, dsk, n_rows=512):
    lc, b, slabs, t, _ = u5.shape
    rows = lc * b
    width = t * LANES
    u4 = u5.reshape(rows, slabs, t, LANES)
    n_rows = min(n_rows, rows)
    kern = functools.partial(_ssm_kernel, batch=b, n_chunks=n_rows // b)
    y4 = pl.pallas_call(
        kern,
        grid=(slabs, rows // n_rows),
        in_specs=[
            pl.BlockSpec((n_rows, None, t, LANES), lambda j, i: (i, j, 0, 0)),
            pl.BlockSpec((t, LANES, width + 2 * SLAB_STATE), lambda j, i: (0, j, 0)),
            pl.BlockSpec((None, 2 * SLAB_STATE, width), lambda j, i: (j, 0, 0)),
            pl.BlockSpec((None, 1, 2 * SLAB_STATE), lambda j, i: (j, 0, 0)),
            pl.BlockSpec((None, 1, width), lambda j, i: (j, 0, 0)),
        ],
        out_specs=pl.BlockSpec((n_rows, None, t, LANES), lambda j, i: (i, j, 0, 0)),
        out_shape=jax.ShapeDtypeStruct((rows, slabs, t, LANES), F32),
        scratch_shapes=[
            pltpu.VMEM((n_rows, 2 * SLAB_STATE), F32),
            pltpu.VMEM((n_rows, 2 * SLAB_STATE), F32),
            pltpu.VMEM((b, 2 * SLAB_STATE), F32),
        ],
        compiler_params=_params(("parallel", "arbitrary")),
        name="ssm_scan",
    )(u4, w_in, cs, a_t, dsk)
    return y4.reshape(lc, b, slabs, t, LANES)


def _glu_out_kernel(y_ref, z_ref, wg_ref, bg_ref, wo_ref, x_ref, gate_ref, g_ref, o_ref):
    slabs = y_ref.shape[1]
    tm = z_ref.shape[0]
    y = jnp.concatenate([y_ref[:, j].reshape(tm, LANES) for j in range(slabs)], axis=-1)
    g = jax.nn.gelu(y)
    t = jnp.dot(g.astype(BF16), wg_ref[...], preferred_element_type=F32) + bg_ref[...]
    y2 = g * jax.nn.sigmoid(t) * jax.nn.silu(z_ref[...].astype(F32))
    o = jnp.dot(y2.astype(BF16), wo_ref[...], preferred_element_type=F32)
    o_ref[...] = x_ref[...] + gate_ref[...] * _rms(o, g_ref[...])


def _glu_out(y5, z, w_glu, b_glu, w_out, x, gate, g_post, tm=512):
    b, l, d = x.shape
    _, _, slabs, t, _ = y5.shape
    width = slabs * LANES
    return pl.pallas_call(
        _glu_out_kernel,
        grid=(b, l // tm),
        in_specs=[
            pl.BlockSpec((tm // t, None, slabs, t, LANES), lambda bi, i: (i, bi, 0, 0, 0)),
            pl.BlockSpec((None, tm, width), lambda bi, i: (bi, i, 0)),
            pl.BlockSpec((width, width), lambda bi, i: (0, 0)),
            pl.BlockSpec((1, width), lambda bi, i: (0, 0)),
            pl.BlockSpec((width, d), lambda bi, i: (0, 0)),
            pl.BlockSpec((None, tm, d), lambda bi, i: (bi, i, 0)),
            pl.BlockSpec((None, 1, d), lambda bi, i: (bi, 0, 0)),
            pl.BlockSpec((1, d), lambda bi, i: (0, 0)),
        ],
        out_specs=pl.BlockSpec((None, tm, d), lambda bi, i: (bi, i, 0)),
        out_shape=jax.ShapeDtypeStruct((b, l, d), F32),
        compiler_params=_params(("parallel", "parallel")),
        name="glu_out",
    )(y5, z, w_glu, b_glu.reshape(1, width), w_out, x, gate.reshape(b, 1, d),
      g_post.reshape(1, d))


def _ab_layer(x, mod, g_pre, g_post, w_in, w_out, sgu_g, sgu_w, sgu_b):
    d = x.shape[-1]
    shift, scale, gate = mod[:, :d], mod[:, d:2 * d], mod[:, 2 * d:]
    wa = d // 2
    heads = wa // HEAD_DIM
    proj = _inproj(x, g_pre, scale, shift, w_in.astype(BF16))
    out_a = _sgu(proj, sgu_w, sgu_b, sgu_g, wa)
    blk = lambda cols: cols // HEAD_DIM
    out_b = _attention(proj, blk(3 * wa), blk(4 * wa), blk(5 * wa), blk(6 * wa), heads)
    return _ab_out(out_a, out_b, w_out.astype(BF16), x, gate, g_post)


def _ssm_layer(x, mod, g_pre, g_post, w_in, w_out, lam_re, lam_im, b_re, b_im, c_re, c_im,
               d_skip, log_dt, w_glu, b_glu):
    d = x.shape[-1]
    shift, scale, gate = mod[:, :d], mod[:, d:2 * d], mod[:, 2 * d:]
    u5, z = _inproj_ssm(x, g_pre, scale, shift, w_in.astype(BF16))
    y5 = _ssm(u5, *_ssm_prep(lam_re, lam_im, log_dt, b_re, b_im, c_re, c_im, d_skip))
    return _glu_out(y5, z, w_glu.astype(BF16), b_glu, w_out.astype(BF16), x, gate, g_post)


def kernel(x, c, ln_pre_g, ln_post_g, w_mod, b_mod, w_in_ab, w_out_ab, sgu_norm_g, sgu_w, sgu_b,
           w_in_ssm, w_out_ssm, lam_re, lam_im, b_re, b_im, c_re, c_im, d_skip, log_dt,
           w_glu, b_glu):
    depth = w_mod.shape[0]
    batch = x.shape[0]
    pad = (-batch) % 8
    c_pad = jnp.pad(c, ((0, pad), (0, 0)))
    mod = _adaln_mod(c_pad, w_mod, b_mod)[:, :batch]
    for layer in range(depth):
        i = layer // 2
        if layer % 2 == 0:
            x = _ab_layer(x, mod[layer], ln_pre_g[layer], ln_post_g[layer], w_in_ab[i],
                          w_out_ab[i], sgu_norm_g[i], sgu_w[i], sgu_b[i])
        else:
            x = _ssm_layer(x, mod[layer], ln_pre_g[layer], ln_post_g[layer], w_in_ssm[i],
                           w_out_ssm[i], lam_re[i], lam_im[i], b_re[i], b_im[i], c_re[i],
                           c_im[i], d_skip[i], log_dt[i], w_glu[i], b_glu[i])
    return x
```

```python
import functools
import math

import jax
import jax.numpy as jnp
from jax import lax
from jax.experimental import pallas as pl
from jax.experimental.pallas import tpu as pltpu

F32 = jnp.float32
BF16 = jnp.bfloat16

EPS = 1e-6
LANES = 128
HEAD_DIM = 128
SGU_CHUNK = 128
SSM_GROUP = 16
SSM_STATE = 64
SSM_T = 8
GROUPS_PER_SLAB = LANES // SSM_GROUP
SLAB_STATE = GROUPS_PER_SLAB * SSM_STATE
VMEM_LIMIT = 56 * 1024 * 1024
EXP_UNDERFLOW = -104.0


def _params(sem, vmem=VMEM_LIMIT):
    return pltpu.CompilerParams(dimension_semantics=sem, vmem_limit_bytes=vmem)


def _rms(y, g):
    ms = jnp.mean(y * y, axis=-1, keepdims=True)
    return y * lax.rsqrt(ms + EPS) * g


def _mod_kernel(c_ref, w_ref, b_ref, o_ref):
    cond = jax.nn.silu(c_ref[...])
    o_ref[...] = jnp.dot(cond.astype(BF16), w_ref[...].astype(BF16),
                         preferred_element_type=F32) + b_ref[...]


def _adaln_mod(c_pad, w_mod, b_mod, tn=512):
    depth, d, n = w_mod.shape
    rows = c_pad.shape[0]
    return pl.pallas_call(
        _mod_kernel,
        grid=(depth, n // tn),
        in_specs=[
            pl.BlockSpec((rows, d), lambda l, j: (0, 0)),
            pl.BlockSpec((None, d, tn), lambda l, j: (l, 0, j)),
            pl.BlockSpec((None, 1, tn), lambda l, j: (l, 0, j)),
        ],
        out_specs=pl.BlockSpec((None, rows, tn), lambda l, j: (l, 0, j)),
        out_shape=jax.ShapeDtypeStruct((depth, rows, n), F32),
        compiler_params=_params(("parallel", "parallel")),
        name="adaln_mod",
    )(c_pad, w_mod, b_mod.reshape(depth, 1, n))


def _prenorm(x_ref, g_ref, sc_ref, sh_ref):
    y = _rms(x_ref[...], g_ref[...])
    return (y * (1.0 + sc_ref[...]) + sh_ref[...]).astype(BF16)


def _inproj_kernel(x_ref, g_ref, sc_ref, sh_ref, w_ref, o_ref, h_scr):
    @pl.when(pl.program_id(2) == 0)
    def _():
        h_scr[...] = _prenorm(x_ref, g_ref, sc_ref, sh_ref)

    o_ref[...] = jnp.dot(h_scr[...], w_ref[...],
                         preferred_element_type=F32).astype(o_ref.dtype)


def _inproj(x, g, scale, shift, w, tm=512, tn=1024):
    b, l, d = x.shape
    n = w.shape[1]
    tn = math.gcd(n, tn)
    return pl.pallas_call(
        _inproj_kernel,
        grid=(b, l // tm, n // tn),
        in_specs=[
            pl.BlockSpec((None, tm, d), lambda bi, i, j: (bi, i, 0)),
            pl.BlockSpec((1, d), lambda bi, i, j: (0, 0)),
            pl.BlockSpec((None, 1, d), lambda bi, i, j: (bi, 0, 0)),
            pl.BlockSpec((None, 1, d), lambda bi, i, j: (bi, 0, 0)),
            pl.BlockSpec((d, tn), lambda bi, i, j: (0, j)),
        ],
        out_specs=pl.BlockSpec((None, tm, tn), lambda bi, i, j: (bi, i, j)),
        out_shape=jax.ShapeDtypeStruct((b, l, n), BF16),
        scratch_shapes=[pltpu.VMEM((tm, d), BF16)],
        compiler_params=_params(("parallel", "parallel", "arbitrary")),
        name="inproj_ab",
    )(x, g.reshape(1, d), scale.reshape(b, 1, d), shift.reshape(b, 1, d), w)


def _inproj_ssm_kernel(x_ref, g_ref, sc_ref, sh_ref, w_ref, u_ref, z_ref, *, width):
    h = _prenorm(x_ref, g_ref, sc_ref, sh_ref)
    acc = jnp.dot(h, w_ref[...], preferred_element_type=F32)
    tm = acc.shape[0]
    for j in range(width // LANES):
        u_ref[:, j] = acc[:, j * LANES:(j + 1) * LANES].reshape(tm // SSM_T, SSM_T, LANES)
    z_ref[...] = acc[:, width:].astype(z_ref.dtype)


def _inproj_ssm(x, g, scale, shift, w, tm=512):
    b, l, d = x.shape
    width = w.shape[1] // 2
    slabs = width // LANES
    kern = functools.partial(_inproj_ssm_kernel, width=width)
    return pl.pallas_call(
        kern,
        grid=(b, l // tm),
        in_specs=[
            pl.BlockSpec((None, tm, d), lambda bi, i: (bi, i, 0)),
            pl.BlockSpec((1, d), lambda bi, i: (0, 0)),
            pl.BlockSpec((None, 1, d), lambda bi, i: (bi, 0, 0)),
            pl.BlockSpec((None, 1, d), lambda bi, i: (bi, 0, 0)),
            pl.BlockSpec((d, 2 * width), lambda bi, i: (0, 0)),
        ],
        out_specs=[
            pl.BlockSpec((tm // SSM_T, None, slabs, SSM_T, LANES),
                         lambda bi, i: (i, bi, 0, 0, 0)),
            pl.BlockSpec((None, tm, width), lambda bi, i: (bi, i, 0)),
        ],
        out_shape=[
            jax.ShapeDtypeStruct((l // SSM_T, b, slabs, SSM_T, LANES), F32),
            jax.ShapeDtypeStruct((b, l, width), BF16),
        ],
        compiler_params=_params(("parallel", "parallel")),
        name="inproj_ssm",
    )(x, g.reshape(1, d), scale.reshape(b, 1, d), shift.reshape(b, 1, d), w)


def _sgu_kernel(u_ref, v_ref, z_ref, w_ref, b_ref, g_ref, o_ref, *, heads, chunks):
    row = lax.broadcasted_iota(jnp.int32, (SGU_CHUNK, SGU_CHUNK), 0)
    col = lax.broadcasted_iota(jnp.int32, (SGU_CHUNK, SGU_CHUNK), 1)
    causal = col <= row
    for h in range(heads):
        w = jnp.where(causal, w_ref[h], 0.0).astype(BF16)
        bias = b_ref[h]
        gain = g_ref[:, h * HEAD_DIM:(h + 1) * HEAD_DIM]
        for c in range(chunks):
            rs = slice(c * SGU_CHUNK, (c + 1) * SGU_CHUNK)
            cs = slice(h * HEAD_DIM, (h + 1) * HEAD_DIM)
            v = jax.nn.gelu(v_ref[rs, cs].astype(F32))
            vn = _rms(v, gain)
            s = jnp.dot(w, vn.astype(BF16), preferred_element_type=F32) + bias
            u = jax.nn.gelu(u_ref[rs, cs].astype(F32))
            z = jax.nn.silu(z_ref[rs, cs].astype(F32))
            o_ref[rs, cs] = (u * s * z).astype(o_ref.dtype)


def _sgu(proj, sgu_w, sgu_b, sgu_g, width, tm=512):
    b, l, _ = proj.shape
    heads = width // HEAD_DIM

    def col_spec(k):
        return pl.BlockSpec((None, tm, width), lambda bi, i: (bi, i, k))

    kern = functools.partial(_sgu_kernel, heads=heads, chunks=tm // SGU_CHUNK)
    return pl.pallas_call(
        kern,
        grid=(b, l // tm),
        in_specs=[
            col_spec(0), col_spec(1), col_spec(2),
            pl.BlockSpec((heads, SGU_CHUNK, SGU_CHUNK), lambda bi, i: (0, 0, 0)),
            pl.BlockSpec((heads, SGU_CHUNK, 1), lambda bi, i: (0, 0, 0)),
            pl.BlockSpec((1, width), lambda bi, i: (0, 0)),
        ],
        out_specs=pl.BlockSpec((None, tm, width), lambda bi, i: (bi, i, 0)),
        out_shape=jax.ShapeDtypeStruct((b, l, width), BF16),
        compiler_params=_params(("parallel", "parallel")),
        name="sgu",
    )(proj, proj, proj, sgu_w, sgu_b.reshape(heads, SGU_CHUNK, 1), sgu_g.reshape(1, width))


def _attn_kernel(q_ref, k_ref, v_ref, bz_ref, o_ref, acc_scr, car_scr,
                 *, tq, tk, sub, heads, scale):
    qi = pl.program_id(2)
    q0 = qi * tq
    n_sub = tk // sub
    r_i = lax.broadcasted_iota(jnp.int32, (sub, sub), 0)
    c_i = lax.broadcasted_iota(jnp.int32, (sub, sub), 1)
    later_mat = (r_i > c_i).astype(BF16)
    q_pos = q0 + lax.broadcasted_iota(jnp.int32, (tq, tk), 0)
    k_off = lax.broadcasted_iota(jnp.int32, (tq, tk), 1)

    acc_scr[...] = jnp.zeros_like(acc_scr)
    car_scr[...] = jnp.zeros_like(car_scr)

    def block(h, ks, limit):
        hc = slice(h * HEAD_DIM, (h + 1) * HEAD_DIM)
        ks = pl.multiple_of(ks, tq)
        q = q_ref[:, hc]
        k = k_ref[pl.ds(ks, tk), hc]
        v = v_ref[pl.ds(ks, tk), hc]
        z = lax.dot_general(q, k, (((1,), (1,)), ((), ())),
                            preferred_element_type=F32) * scale
        soft = jnp.log(1.0 + jnp.exp(-jnp.abs(z)))
        log_beta = jnp.minimum(z, 0.0) - soft
        log_keep = log_beta - z
        if limit is not None:
            mask = (ks + k_off) < limit
            log_keep = jnp.where(mask, log_keep, 0.0)
        carry = car_scr[h]
        ws = [None] * n_sub
        for i in reversed(range(n_sub)):
            cols = slice(i * sub, (i + 1) * sub)
            lk = log_keep[:, cols]
            hi = lk.astype(BF16)
            lo = (lk - hi.astype(F32)).astype(BF16)
            later = (jnp.dot(hi, later_mat, preferred_element_type=F32)
                     + jnp.dot(lo, later_mat, preferred_element_type=F32))
            ws[i] = jnp.exp(log_beta[:, cols] + later + carry)
            carry = carry + later[:, :1] + lk[:, :1]
        w = jnp.concatenate(ws, axis=1)
        if limit is not None:
            w = jnp.where(mask, w, 0.0)
        acc_scr[h] += jnp.dot(w.astype(BF16), v, preferred_element_type=F32)
        car_scr[h] = carry

    ks0 = jnp.maximum(q0 + tq - tk, 0)
    for h in range(heads):
        block(h, ks0, q_pos)
    n_full = ks0 // tk
    rest = ks0 - n_full * tk

    for h in range(heads):
        def live(h=h):
            return jnp.max(car_scr[h]) > EXP_UNDERFLOW

        def cond(state):
            it, alive = state
            return jnp.logical_and(it < n_full, alive)

        def body(state, h=h, live=live):
            it, _ = state
            block(h, ks0 - (it + 1) * tk, None)
            return it + 1, live()

        _, alive = lax.while_loop(cond, body, (jnp.int32(0), live()))

        @pl.when(jnp.logical_and(rest > 0, alive))
        def _(h=h):
            block(h, 0, rest)

    gate = jax.nn.silu(bz_ref[...].astype(F32))
    acc = jnp.concatenate([acc_scr[h] for h in range(heads)], axis=-1)
    o_ref[...] = (acc * gate).astype(o_ref.dtype)


def _attention(proj, q_col, k_col, v_col, z_col, n_heads, tq=256, tk=512, sub=256, heads=2):
    b, l, _ = proj.shape
    hw = heads * HEAD_DIM
    kern = functools.partial(_attn_kernel, tq=tq, tk=tk, sub=sub, heads=heads,
                             scale=1.0 / math.sqrt(HEAD_DIM))
    return pl.pallas_call(
        kern,
        grid=(b, n_heads // heads, l // tq),
        in_specs=[
            pl.BlockSpec((None, tq, hw), lambda bi, h, i: (bi, i, q_col + h)),
            pl.BlockSpec((None, l, hw), lambda bi, h, i: (bi, 0, k_col + h)),
            pl.BlockSpec((None, l, hw), lambda bi, h, i: (bi, 0, v_col + h)),
            pl.BlockSpec((None, tq, hw), lambda bi, h, i: (bi, i, z_col + h)),
        ],
        out_specs=pl.BlockSpec((None, tq, hw), lambda bi, h, i: (bi, i, h)),
        out_shape=jax.ShapeDtypeStruct((b, l, n_heads * HEAD_DIM), BF16),
        scratch_shapes=[pltpu.VMEM((heads, tq, HEAD_DIM), F32), pltpu.VMEM((heads, tq, 1), F32)],
        compiler_params=_params(("parallel", "parallel", "parallel")),
        name="stickbreak_attn",
    )(proj, proj, proj, proj)


def _ab_out_kernel(a_ref, b_ref, wa_ref, wb_ref, x_ref, gate_ref, g_ref, o_ref):
    y = (jnp.dot(a_ref[...], wa_ref[...], preferred_element_type=F32)
         + jnp.dot(b_ref[...], wb_ref[...], preferred_element_type=F32))
    o_ref[...] = x_ref[...] + gate_ref[...] * _rms(y, g_ref[...])


def _ab_out(out_a, out_b, w_out, x, gate, g_post, tm=512):
    b, l, d = x.shape
    wa = out_a.shape[-1]
    wb = out_b.shape[-1]
    assert wa == wb
    return pl.pallas_call(
        _ab_out_kernel,
        grid=(b, l // tm),
        in_specs=[
            pl.BlockSpec((None, tm, wa), lambda bi, i: (bi, i, 0)),
            pl.BlockSpec((None, tm, wb), lambda bi, i: (bi, i, 0)),
            pl.BlockSpec((wa, d), lambda bi, i: (0, 0)),
            pl.BlockSpec((wb, d), lambda bi, i: (1, 0)),
            pl.BlockSpec((None, tm, d), lambda bi, i: (bi, i, 0)),
            pl.BlockSpec((None, 1, d), lambda bi, i: (bi, 0, 0)),
            pl.BlockSpec((1, d), lambda bi, i: (0, 0)),
        ],
        out_specs=pl.BlockSpec((None, tm, d), lambda bi, i: (bi, i, 0)),
        out_shape=jax.ShapeDtypeStruct((b, l, d), F32),
        compiler_params=_params(("parallel", "parallel")),
        name="ab_out",
    )(out_a, out_b, w_out, w_out, x, gate.reshape(b, 1, d), g_post.reshape(1, d))


def _discretise(lr, li, ldt):
    dt = jnp.exp(ldt)
    mag = jnp.exp(lr * dt)
    return mag * jnp.cos(li * dt), mag * jnp.sin(li * dt)


def _cmul(x_re, x_im, y_re, y_im):
    return x_re * y_re - x_im * y_im, x_re * y_im + x_im * y_re


def _ssm_prep_kernel(lrc_ref, lic_ref, ldc_ref, br_ref, bi_ref, cr_ref, ci_ref,
                     lrp_ref, lip_ref, ldp_ref, ctr_ref, cti_ref,
                     lrl_ref, lil_ref, ldl_ref,
                     w_ref, cs_ref, at_ref):
    t_steps = SSM_T
    ns = SLAB_STATE
    out_w = t_steps * LANES
    nt = (((1,), (1,)), ((), ()))
    hp = lax.Precision.HIGHEST
    lr = lrc_ref[...]
    li = lic_ref[...]
    a_re, a_im = _discretise(lr, li, ldc_ref[...])
    den = lr * lr + li * li
    nr = a_re - 1.0
    coef_re = (nr * lr + a_im * li) / den
    coef_im = (a_im * lr - nr * li) / den
    bb_re, bb_im = _cmul(coef_re, coef_im, br_ref[...], bi_ref[...])
    row_g = lax.broadcasted_iota(jnp.int32, lr.shape, 0) // SSM_GROUP
    col_g = lax.broadcasted_iota(jnp.int32, lr.shape, 1) // SSM_STATE
    same_state = row_g == col_g
    row_k = lax.broadcasted_iota(jnp.int32, (LANES, LANES), 0) // SSM_GROUP
    col_k = lax.broadcasted_iota(jnp.int32, (LANES, LANES), 1) // SSM_GROUP
    same_lag = row_k == col_k
    c_re = cr_ref[...]
    c_im = ci_ref[...]
    a64_re = a_re[:, :SSM_STATE]
    a64_im = a_im[:, :SSM_STATE]
    bb64_re = bb_re[:, :SSM_STATE]
    bb64_im = bb_im[:, :SSM_STATE]
    p_re = jnp.ones_like(lr)
    p_im = jnp.zeros_like(lr)
    q_re = jnp.ones_like(a64_re)
    q_im = jnp.zeros_like(a64_re)
    lag_blocks = []
    for d in range(t_steps):
        s = t_steps - 1 - d
        bs_re, bs_im = _cmul(bb_re, bb_im, p_re, p_im)
        w_ref[s, :, out_w:out_w + ns] = jnp.where(same_state, bs_re, 0.0).astype(w_ref.dtype)
        w_ref[s, :, out_w + ns:] = jnp.where(same_state, bs_im, 0.0).astype(w_ref.dtype)
        cd_re, cd_im = _cmul(c_re, c_im, q_re, q_im)
        kd = (lax.dot_general(bb64_re, cd_re, nt, precision=hp, preferred_element_type=F32)
              - lax.dot_general(bb64_im, cd_im, nt, precision=hp, preferred_element_type=F32))
        lag_blocks.append(jnp.where(same_lag, kd, 0.0).astype(w_ref.dtype))
        p_re, p_im = _cmul(p_re, p_im, a_re, a_im)
        q_re, q_im = _cmul(q_re, q_im, a64_re, a64_im)
    zero_block = jnp.zeros((LANES, LANES), w_ref.dtype)
    for s in range(t_steps):
        for t in range(t_steps):
            w_ref[s, :, t * LANES:(t + 1) * LANES] = lag_blocks[t - s] if t >= s else zero_block
    ap_re, ap_im = _discretise(lrp_ref[...], lip_ref[...], ldp_ref[...])
    row_p = lax.broadcasted_iota(jnp.int32, ap_re.shape, 0) // SSM_STATE
    col_p = lax.broadcasted_iota(jnp.int32, ap_re.shape, 1) // SSM_GROUP
    same_out = row_p == col_p
    ct_re = ctr_ref[...]
    ct_im = cti_ref[...]
    r_re, r_im = ap_re, ap_im
    for t in range(t_steps):
        cd_re, cd_im = _cmul(ct_re, ct_im, r_re, r_im)
        cols = slice(t * LANES, (t + 1) * LANES)
        cs_ref[:ns, cols] = jnp.where(same_out, cd_re, 0.0).astype(cs_ref.dtype)
        cs_ref[ns:, cols] = jnp.where(same_out, -cd_im, 0.0).astype(cs_ref.dtype)
        r_re, r_im = _cmul(r_re, r_im, ap_re, ap_im)
    al_re, al_im = _discretise(lrl_ref[...], lil_ref[...], ldl_ref[...])
    t_re, t_im = al_re, al_im
    for _ in range(t_steps - 1):
        t_re, t_im = _cmul(t_re, t_im, al_re, al_im)
    at_ref[:, :ns] = t_re
    at_ref[:, ns:] = t_im


def _ssm_prep(lam_re, lam_im, log_dt, b_re, b_im, c_re, c_im, d_skip):
    g, p = lam_re.shape
    gs = GROUPS_PER_SLAB
    rows = g * SSM_GROUP
    slabs = rows // LANES
    ns = SLAB_STATE
    width = SSM_T * LANES
    ldt = jnp.broadcast_to(log_dt.reshape(g, 1), (g, p))
    chan = lambda a: jnp.tile(jnp.repeat(a, SSM_GROUP, axis=0), (1, gs))
    b_t = lambda a: jnp.tile(jnp.transpose(a, (0, 2, 1)).reshape(rows, p), (1, gs))
    stat = lambda a: jnp.broadcast_to(a.reshape(g * p, 1), (g * p, LANES))
    c_t = lambda a: jnp.tile(jnp.transpose(a, (0, 2, 1)).reshape(g * p, SSM_GROUP), (1, gs))
    lane = lambda a: a.reshape(slabs, 1, ns)
    chan_spec = pl.BlockSpec((LANES, ns), lambda j: (j, 0))
    stat_spec = pl.BlockSpec((ns, LANES), lambda j: (j, 0))
    lane_spec = pl.BlockSpec((None, 1, ns), lambda j: (j, 0, 0))
    w_in, cs, a_t = pl.pallas_call(
        _ssm_prep_kernel,
        grid=(slabs,),
        in_specs=[chan_spec] * 5 + [pl.BlockSpec((LANES, p), lambda j: (j, 0))] * 2
                 + [stat_spec] * 5 + [lane_spec] * 3,
        out_specs=[
            pl.BlockSpec((SSM_T, LANES, width + 2 * ns), lambda j: (0, j, 0)),
            pl.BlockSpec((None, 2 * ns, width), lambda j: (j, 0, 0)),
            pl.BlockSpec((None, 1, 2 * ns), lambda j: (j, 0, 0)),
        ],
        out_shape=[
            jax.ShapeDtypeStruct((SSM_T, rows, width + 2 * ns), BF16),
            jax.ShapeDtypeStruct((slabs, 2 * ns, width), BF16),
            jax.ShapeDtypeStruct((slabs, 1, 2 * ns), F32),
        ],
        compiler_params=_params(("parallel",)),
        name="ssm_prep",
    )(chan(lam_re), chan(lam_im), chan(ldt), b_t(b_re), b_t(b_im),
      c_re.reshape(rows, p), c_im.reshape(rows, p),
      stat(lam_re), stat(lam_im), stat(ldt), c_t(c_re), c_t(c_im),
      lane(lam_re), lane(lam_im), lane(ldt))
    dsk = jnp.tile(d_skip.reshape(slabs, 1, LANES), (1, 1, SSM_T))
    return w_in, cs, a_t, dsk


def _ssm_kernel(u_ref, w_ref, cs_ref, at_ref, dsk_ref, y_ref, s_scr, hp_scr, h_scr,
                *, batch, n_chunks):
    ns = SLAB_STATE
    out_w = SSM_T * LANES

    @pl.when(pl.program_id(1) == 0)
    def _():
        h_scr[...] = jnp.zeros_like(h_scr)

    u = jnp.concatenate([u_ref[:, s, :] for s in range(SSM_T)], axis=-1)
    w = w_ref[...].reshape(out_w, out_w + 2 * ns)
    r = jnp.dot(u.astype(BF16), w, preferred_element_type=F32)
    s_scr[...] = r[:, out_w:]
    a_re = jnp.broadcast_to(at_ref[:, :ns], (batch, ns))
    a_im = jnp.broadcast_to(at_ref[:, ns:], (batch, ns))

    sub = 8
    per_tile = sub // batch

    def step(k, carry):
        h_re, h_im = carry
        r0 = pl.multiple_of(k * sub, sub)
        s_re = s_scr[pl.ds(r0, sub), :ns]
        s_im = s_scr[pl.ds(r0, sub), ns:]
        prev_re, prev_im = [], []
        for c in range(per_tile):
            rows = slice(c * batch, (c + 1) * batch)
            prev_re.append(h_re)
            prev_im.append(h_im)
            h_re, h_im = (a_re * h_re - a_im * h_im + s_re[rows],
                          a_re * h_im + a_im * h_re + s_im[rows])
        hp_scr[pl.ds(r0, sub), :ns] = jnp.concatenate(prev_re, axis=0)
        hp_scr[pl.ds(r0, sub), ns:] = jnp.concatenate(prev_im, axis=0)
        return h_re, h_im

    h_re, h_im = lax.fori_loop(0, n_chunks // per_tile, step,
                               (h_scr[:, :ns], h_scr[:, ns:]))
    h_scr[:, :ns] = h_re
    h_scr[:, ns:] = h_im
    y = (r[:, :out_w]
         + jnp.dot(hp_scr[...].astype(BF16), cs_ref[...], preferred_element_type=F32)
         + dsk_ref[...] * u)
    for t in range(SSM_T):
        y_ref[:, t, :] = y[:, t * LANES:(t + 1) * LANES]


def _ssm(u5, w_in, cs, a_t, dsk, n_rows=512):
    lc, b, slabs, t, _ = u5.shape
    rows = lc * b
    width = t * LANES
    u4 = u5.reshape(rows, slabs, t, LANES)
    n_rows = min(n_rows, rows)
    kern = functools.partial(_ssm_kernel, batch=b, n_chunks=n_rows // b)
    y4 = pl.pallas_call(
        kern,
        grid=(slabs, rows // n_rows),
        in_specs=[
            pl.BlockSpec((n_rows, None, t, LANES), lambda j, i: (i, j, 0, 0)),
            pl.BlockSpec((t, LANES, width + 2 * SLAB_STATE), lambda j, i: (0, j, 0)),
            pl.BlockSpec((None, 2 * SLAB_STATE, width), lambda j, i: (j, 0, 0)),
            pl.BlockSpec((None, 1, 2 * SLAB_STATE), lambda j, i: (j, 0, 0)),
            pl.BlockSpec((None, 1, width), lambda j, i: (j, 0, 0)),
        ],
        out_specs=pl.BlockSpec((n_rows, None, t, LANES), lambda j, i: (i, j, 0, 0)),
        out_shape=jax.ShapeDtypeStruct((rows, slabs, t, LANES), F32),
        scratch_shapes=[
            pltpu.VMEM((n_rows, 2 * SLAB_STATE), F32),
            pltpu.VMEM((n_rows, 2 * SLAB_STATE), F32),
            pltpu.VMEM((b, 2 * SLAB_STATE), F32),
        ],
        compiler_params=_params(("parallel", "arbitrary")),
        name="ssm_scan",
    )(u4, w_in, cs, a_t, dsk)
    return y4.reshape(lc, b, slabs, t, LANES)


def _glu_out_kernel(y_ref, z_ref, wg_ref, bg_ref, wo_ref, x_ref, gate_ref, g_ref, o_ref):
    slabs = y_ref.shape[1]
    tm = z_ref.shape[0]
    y = jnp.concatenate([y_ref[:, j].reshape(tm, LANES) for j in range(slabs)], axis=-1)
    g = jax.nn.gelu(y)
    t = jnp.dot(g.astype(BF16), wg_ref[...], preferred_element_type=F32) + bg_ref[...]
    y2 = g * jax.nn.sigmoid(t) * jax.nn.silu(z_ref[...].astype(F32))
    o = jnp.dot(y2.astype(BF16), wo_ref[...], preferred_element_type=F32)
    o_ref[...] = x_ref[...] + gate_ref[...] * _rms(o, g_ref[...])


def _glu_out(y5, z, w_glu, b_glu, w_out, x, gate, g_post, tm=512):
    b, l, d = x.shape
    _, _, slabs, t, _ = y5.shape
    width = slabs * LANES
    return pl.pallas_call(
        _glu_out_kernel,
        grid=(b, l // tm),
        in_specs=[
            pl.BlockSpec((tm // t, None, slabs, t, LANES), lambda bi, i: (i, bi, 0, 0, 0)),
            pl.BlockSpec((None, tm, width), lambda bi, i: (bi, i, 0)),
            pl.BlockSpec((width, width), lambda bi, i: (0, 0)),
            pl.BlockSpec((1, width), lambda bi, i: (0, 0)),
            pl.BlockSpec((width, d), lambda bi, i: (0, 0)),
            pl.BlockSpec((None, tm, d), lambda bi, i: (bi, i, 0)),
            pl.BlockSpec((None, 1, d), lambda bi, i: (bi, 0, 0)),
            pl.BlockSpec((1, d), lambda bi, i: (0, 0)),
        ],
        out_specs=pl.BlockSpec((None, tm, d), lambda bi, i: (bi, i, 0)),
        out_shape=jax.ShapeDtypeStruct((b, l, d), F32),
        compiler_params=_params(("parallel", "parallel")),
        name="glu_out",
    )(y5, z, w_glu, b_glu.reshape(1, width), w_out, x, gate.reshape(b, 1, d),
      g_post.reshape(1, d))


def _ab_layer(x, mod, g_pre, g_post, w_in, w_out, sgu_g, sgu_w, sgu_b):
    d = x.shape[-1]
    shift, scale, gate = mod[:, :d], mod[:, d:2 * d], mod[:, 2 * d:]
    wa = d // 2
    heads = wa // HEAD_DIM
    proj = _inproj(x, g_pre, scale, shift, w_in.astype(BF16))
    out_a = _sgu(proj, sgu_w, sgu_b, sgu_g, wa)
    attn_heads = 2
    blk = lambda cols: cols // (attn_heads * HEAD_DIM)
    out_b = _attention(proj, blk(3 * wa), blk(4 * wa), blk(5 * wa), blk(6 * wa), heads,
                       heads=attn_heads)
    return _ab_out(out_a, out_b, w_out.astype(BF16), x, gate, g_post)


def _ssm_layer(x, mod, g_pre, g_post, w_in, w_out, lam_re, lam_im, b_re, b_im, c_re, c_im,
               d_skip, log_dt, w_glu, b_glu):
    d = x.shape[-1]
    shift, scale, gate = mod[:, :d], mod[:, d:2 * d], mod[:, 2 * d:]
    u5, z = _inproj_ssm(x, g_pre, scale, shift, w_in.astype(BF16))
    y5 = _ssm(u5, *_ssm_prep(lam_re, lam_im, log_dt, b_re, b_im, c_re, c_im, d_skip))
    return _glu_out(y5, z, w_glu.astype(BF16), b_glu, w_out.astype(BF16), x, gate, g_post)


def kernel(x, c, ln_pre_g, ln_post_g, w_mod, b_mod, w_in_ab, w_out_ab, sgu_norm_g, sgu_w, sgu_b,
           w_in_ssm, w_out_ssm, lam_re, lam_im, b_re, b_im, c_re, c_im, d_skip, log_dt,
           w_glu, b_glu):
    depth = w_mod.shape[0]
    batch = x.shape[0]
    pad = (-batch) % 8
    c_pad = jnp.pad(c, ((0, pad), (0, 0)))
    mod = _adaln_mod(c_pad, w_mod, b_mod)[:, :batch]
    for layer in range(depth):
        i = layer // 2
        if layer % 2 == 0:
            x = _ab_layer(x, mod[layer], ln_pre_g[layer], ln_post_g[layer], w_in_ab[i],
                          w_out_ab[i], sgu_norm_g[i], sgu_w[i], sgu_b[i])
        else:
            x = _ssm_layer(x, mod[layer], ln_pre_g[layer], ln_post_g[layer], w_in_ssm[i],
                           w_out_ssm[i], lam_re[i], lam_im[i], b_re[i], b_im[i], c_re[i],
                           c_im[i], d_skip[i], log_dt[i], w_glu[i], b_glu[i])
    return x
```

```python
import functools
import math

import jax
import jax.numpy as jnp
from jax import lax
from jax.experimental import pallas as pl
from jax.experimental.pallas import tpu as pltpu

F32 = jnp.float32
BF16 = jnp.bfloat16

EPS = 1e-6
LANES = 128
HEAD_DIM = 128
SGU_CHUNK = 128
SSM_GROUP = 16
SSM_STATE = 64
SSM_T = 8
GROUPS_PER_SLAB = LANES // SSM_GROUP
SLAB_STATE = GROUPS_PER_SLAB * SSM_STATE
VMEM_LIMIT = 56 * 1024 * 1024
EXP_UNDERFLOW = -104.0


def _params(sem, vmem=VMEM_LIMIT):
    return pltpu.CompilerParams(dimension_semantics=sem, vmem_limit_bytes=vmem)


def _rms(y, g):
    ms = jnp.mean(y * y, axis=-1, keepdims=True)
    return y * lax.rsqrt(ms + EPS) * g


def _mod_kernel(c_ref, w_ref, b_ref, o_ref):
    cond = jax.nn.silu(c_ref[...])
    o_ref[...] = jnp.dot(cond.astype(BF16), w_ref[...].astype(BF16),
                         preferred_element_type=F32) + b_ref[...]


def _adaln_mod(c_pad, w_mod, b_mod, tn=512):
    depth, d, n = w_mod.shape
    rows = c_pad.shape[0]
    return pl.pallas_call(
        _mod_kernel,
        grid=(depth, n // tn),
        in_specs=[
            pl.BlockSpec((rows, d), lambda l, j: (0, 0)),
            pl.BlockSpec((None, d, tn), lambda l, j: (l, 0, j)),
            pl.BlockSpec((None, 1, tn), lambda l, j: (l, 0, j)),
        ],
        out_specs=pl.BlockSpec((None, rows, tn), lambda l, j: (l, 0, j)),
        out_shape=jax.ShapeDtypeStruct((depth, rows, n), F32),
        compiler_params=_params(("parallel", "parallel")),
        name="adaln_mod",
    )(c_pad, w_mod, b_mod.reshape(depth, 1, n))


def _prenorm(x_ref, g_ref, sc_ref, sh_ref):
    y = _rms(x_ref[...], g_ref[...])
    return (y * (1.0 + sc_ref[...]) + sh_ref[...]).astype(BF16)


def _inproj_kernel(x_ref, g_ref, sc_ref, sh_ref, w_ref, o_ref, h_scr):
    @pl.when(pl.program_id(2) == 0)
    def _():
        h_scr[...] = _prenorm(x_ref, g_ref, sc_ref, sh_ref)

    o_ref[...] = jnp.dot(h_scr[...], w_ref[...],
                         preferred_element_type=F32).astype(o_ref.dtype)


def _inproj(x, g, scale, shift, w, tm=1024, tn=1024):
    b, l, d = x.shape
    n = w.shape[1]
    tn = math.gcd(n, tn)
    return pl.pallas_call(
        _inproj_kernel,
        grid=(b, l // tm, n // tn),
        in_specs=[
            pl.BlockSpec((None, tm, d), lambda bi, i, j: (bi, i, 0)),
            pl.BlockSpec((1, d), lambda bi, i, j: (0, 0)),
            pl.BlockSpec((None, 1, d), lambda bi, i, j: (bi, 0, 0)),
            pl.BlockSpec((None, 1, d), lambda bi, i, j: (bi, 0, 0)),
            pl.BlockSpec((d, tn), lambda bi, i, j: (0, j)),
        ],
        out_specs=pl.BlockSpec((None, tm, tn), lambda bi, i, j: (bi, i, j)),
        out_shape=jax.ShapeDtypeStruct((b, l, n), BF16),
        scratch_shapes=[pltpu.VMEM((tm, d), BF16)],
        compiler_params=_params(("parallel", "parallel", "arbitrary")),
        name="inproj_ab",
    )(x, g.reshape(1, d), scale.reshape(b, 1, d), shift.reshape(b, 1, d), w)


def _inproj_ssm_kernel(x_ref, g_ref, sc_ref, sh_ref, w_ref, u_ref, z_ref, *, width):
    h = _prenorm(x_ref, g_ref, sc_ref, sh_ref)
    acc = jnp.dot(h, w_ref[...], preferred_element_type=F32)
    tm = acc.shape[0]
    for j in range(width // LANES):
        u_ref[:, j] = acc[:, j * LANES:(j + 1) * LANES].reshape(tm // SSM_T, SSM_T, LANES)
    z_ref[...] = acc[:, width:].astype(z_ref.dtype)


def _inproj_ssm(x, g, scale, shift, w, tm=512):
    b, l, d = x.shape
    width = w.shape[1] // 2
    slabs = width // LANES
    kern = functools.partial(_inproj_ssm_kernel, width=width)
    return pl.pallas_call(
        kern,
        grid=(b, l // tm),
        in_specs=[
            pl.BlockSpec((None, tm, d), lambda bi, i: (bi, i, 0)),
            pl.BlockSpec((1, d), lambda bi, i: (0, 0)),
            pl.BlockSpec((None, 1, d), lambda bi, i: (bi, 0, 0)),
            pl.BlockSpec((None, 1, d), lambda bi, i: (bi, 0, 0)),
            pl.BlockSpec((d, 2 * width), lambda bi, i: (0, 0)),
        ],
        out_specs=[
            pl.BlockSpec((tm // SSM_T, None, slabs, SSM_T, LANES),
                         lambda bi, i: (i, bi, 0, 0, 0)),
            pl.BlockSpec((None, tm, width), lambda bi, i: (bi, i, 0)),
        ],
        out_shape=[
            jax.ShapeDtypeStruct((l // SSM_T, b, slabs, SSM_T, LANES), F32),
            jax.ShapeDtypeStruct((b, l, width), BF16),
        ],
        compiler_params=_params(("parallel", "parallel")),
        name="inproj_ssm",
    )(x, g.reshape(1, d), scale.reshape(b, 1, d), shift.reshape(b, 1, d), w)


def _sgu_kernel(u_ref, v_ref, z_ref, w_ref, b_ref, g_ref, o_ref, *, heads, chunks):
    row = lax.broadcasted_iota(jnp.int32, (SGU_CHUNK, SGU_CHUNK), 0)
    col = lax.broadcasted_iota(jnp.int32, (SGU_CHUNK, SGU_CHUNK), 1)
    causal = col <= row
    for h in range(heads):
        w = jnp.where(causal, w_ref[h], 0.0).astype(BF16)
        bias = b_ref[h]
        gain = g_ref[:, h * HEAD_DIM:(h + 1) * HEAD_DIM]
        for c in range(chunks):
            rs = slice(c * SGU_CHUNK, (c + 1) * SGU_CHUNK)
            cs = slice(h * HEAD_DIM, (h + 1) * HEAD_DIM)
            v = jax.nn.gelu(v_ref[rs, cs].astype(F32))
            vn = _rms(v, gain)
            s = jnp.dot(w, vn.astype(BF16), preferred_element_type=F32) + bias
            u = jax.nn.gelu(u_ref[rs, cs].astype(F32))
            z = jax.nn.silu(z_ref[rs, cs].astype(F32))
            o_ref[rs, cs] = (u * s * z).astype(o_ref.dtype)


def _sgu(proj, sgu_w, sgu_b, sgu_g, width, tm=512):
    b, l, _ = proj.shape
    heads = width // HEAD_DIM

    def col_spec(k):
        return pl.BlockSpec((None, tm, width), lambda bi, i: (bi, i, k))

    kern = functools.partial(_sgu_kernel, heads=heads, chunks=tm // SGU_CHUNK)
    return pl.pallas_call(
        kern,
        grid=(b, l // tm),
        in_specs=[
            col_spec(0), col_spec(1), col_spec(2),
            pl.BlockSpec((heads, SGU_CHUNK, SGU_CHUNK), lambda bi, i: (0, 0, 0)),
            pl.BlockSpec((heads, SGU_CHUNK, 1), lambda bi, i: (0, 0, 0)),
            pl.BlockSpec((1, width), lambda bi, i: (0, 0)),
        ],
        out_specs=pl.BlockSpec((None, tm, width), lambda bi, i: (bi, i, 0)),
        out_shape=jax.ShapeDtypeStruct((b, l, width), BF16),
        compiler_params=_params(("parallel", "parallel")),
        name="sgu",
    )(proj, proj, proj, sgu_w, sgu_b.reshape(heads, SGU_CHUNK, 1), sgu_g.reshape(1, width))


def _attn_kernel(q_ref, k_ref, v_ref, bz_ref, o_ref, acc_scr, car_scr,
                 *, tq, tk, sub, heads, scale):
    qi = pl.program_id(2)
    q0 = qi * tq
    n_sub = tk // sub
    r_i = lax.broadcasted_iota(jnp.int32, (sub, sub), 0)
    c_i = lax.broadcasted_iota(jnp.int32, (sub, sub), 1)
    later_mat = (r_i > c_i).astype(BF16)
    q_pos = q0 + lax.broadcasted_iota(jnp.int32, (tq, tk), 0)
    k_off = lax.broadcasted_iota(jnp.int32, (tq, tk), 1)

    acc_scr[...] = jnp.zeros_like(acc_scr)
    car_scr[...] = jnp.zeros_like(car_scr)

    def block(h, ks, limit):
        hc = slice(h * HEAD_DIM, (h + 1) * HEAD_DIM)
        ks = pl.multiple_of(ks, tq)
        q = q_ref[:, hc]
        k = k_ref[pl.ds(ks, tk), hc]
        v = v_ref[pl.ds(ks, tk), hc]
        z = lax.dot_general(q, k, (((1,), (1,)), ((), ())),
                            preferred_element_type=F32) * scale
        soft = jnp.log(1.0 + jnp.exp(-jnp.abs(z)))
        log_beta = jnp.minimum(z, 0.0) - soft
        log_keep = log_beta - z
        if limit is not None:
            mask = (ks + k_off) < limit
            log_keep = jnp.where(mask, log_keep, 0.0)
        carry = car_scr[h]
        ws = [None] * n_sub
        for i in reversed(range(n_sub)):
            cols = slice(i * sub, (i + 1) * sub)
            lk = log_keep[:, cols]
            hi = lk.astype(BF16)
            lo = (lk - hi.astype(F32)).astype(BF16)
            later = (jnp.dot(hi, later_mat, preferred_element_type=F32)
                     + jnp.dot(lo, later_mat, preferred_element_type=F32))
            ws[i] = jnp.exp(log_beta[:, cols] + later + carry)
            carry = carry + later[:, :1] + lk[:, :1]
        w = jnp.concatenate(ws, axis=1)
        if limit is not None:
            w = jnp.where(mask, w, 0.0)
        acc_scr[h] += jnp.dot(w.astype(BF16), v, preferred_element_type=F32)
        car_scr[h] = carry

    ks0 = jnp.maximum(q0 + tq - tk, 0)
    for h in range(heads):
        block(h, ks0, q_pos)
    n_full = ks0 // tk
    rest = ks0 - n_full * tk

    for h in range(heads):
        def live(h=h):
            return jnp.max(car_scr[h]) > EXP_UNDERFLOW

        def cond(state):
            it, alive = state
            return jnp.logical_and(it < n_full, alive)

        def body(state, h=h, live=live):
            it, _ = state
            block(h, ks0 - (it + 1) * tk, None)
            return it + 1, live()

        _, alive = lax.while_loop(cond, body, (jnp.int32(0), live()))

        @pl.when(jnp.logical_and(rest > 0, alive))
        def _(h=h):
            block(h, 0, rest)

    gate = jax.nn.silu(bz_ref[...].astype(F32))
    acc = jnp.concatenate([acc_scr[h] for h in range(heads)], axis=-1)
    o_ref[...] = (acc * gate).astype(o_ref.dtype)


def _attention(proj, q_col, k_col, v_col, z_col, n_heads, tq=256, tk=512, sub=256, heads=2):
    b, l, _ = proj.shape
    hw = heads * HEAD_DIM
    kern = functools.partial(_attn_kernel, tq=tq, tk=tk, sub=sub, heads=heads,
                             scale=1.0 / math.sqrt(HEAD_DIM))
    return pl.pallas_call(
        kern,
        grid=(b, n_heads // heads, l // tq),
        in_specs=[
            pl.BlockSpec((None, tq, hw), lambda bi, h, i: (bi, i, q_col + h)),
            pl.BlockSpec((None, l, hw), lambda bi, h, i: (bi, 0, k_col + h)),
            pl.BlockSpec((None, l, hw), lambda bi, h, i: (bi, 0, v_col + h)),
            pl.BlockSpec((None, tq, hw), lambda bi, h, i: (bi, i, z_col + h)),
        ],
        out_specs=pl.BlockSpec((None, tq, hw), lambda bi, h, i: (bi, i, h)),
        out_shape=jax.ShapeDtypeStruct((b, l, n_heads * HEAD_DIM), BF16),
        scratch_shapes=[pltpu.VMEM((heads, tq, HEAD_DIM), F32), pltpu.VMEM((heads, tq, 1), F32)],
        compiler_params=_params(("parallel", "parallel", "parallel")),
        name="stickbreak_attn",
    )(proj, proj, proj, proj)


def _ab_out_kernel(a_ref, b_ref, wa_ref, wb_ref, x_ref, gate_ref, g_ref, o_ref):
    y = (jnp.dot(a_ref[...], wa_ref[...], preferred_element_type=F32)
         + jnp.dot(b_ref[...], wb_ref[...], preferred_element_type=F32))
    o_ref[...] = x_ref[...] + gate_ref[...] * _rms(y, g_ref[...])


def _ab_out(out_a, out_b, w_out, x, gate, g_post, tm=512):
    b, l, d = x.shape
    wa = out_a.shape[-1]
    wb = out_b.shape[-1]
    assert wa == wb
    return pl.pallas_call(
        _ab_out_kernel,
        grid=(b, l // tm),
        in_specs=[
            pl.BlockSpec((None, tm, wa), lambda bi, i: (bi, i, 0)),
            pl.BlockSpec((None, tm, wb), lambda bi, i: (bi, i, 0)),
            pl.BlockSpec((wa, d), lambda bi, i: (0, 0)),
            pl.BlockSpec((wb, d), lambda bi, i: (1, 0)),
            pl.BlockSpec((None, tm, d), lambda bi, i: (bi, i, 0)),
            pl.BlockSpec((None, 1, d), lambda bi, i: (bi, 0, 0)),
            pl.BlockSpec((1, d), lambda bi, i: (0, 0)),
        ],
        out_specs=pl.BlockSpec((None, tm, d), lambda bi, i: (bi, i, 0)),
        out_shape=jax.ShapeDtypeStruct((b, l, d), F32),
        compiler_params=_params(("parallel", "parallel")),
        name="ab_out",
    )(out_a, out_b, w_out, w_out, x, gate.reshape(b, 1, d), g_post.reshape(1, d))


def _discretise(lr, li, ldt):
    dt = jnp.exp(ldt)
    mag = jnp.exp(lr * dt)
    return mag * jnp.cos(li * dt), mag * jnp.sin(li * dt)


def _cmul(x_re, x_im, y_re, y_im):
    return x_re * y_re - x_im * y_im, x_re * y_im + x_im * y_re


def _ssm_prep_kernel(lrc_ref, lic_ref, ldc_ref, br_ref, bi_ref, cr_ref, ci_ref,
                     lrp_ref, lip_ref, ldp_ref, ctr_ref, cti_ref,
                     lrl_ref, lil_ref, ldl_ref,
                     w_ref, cs_ref, at_ref):
    t_steps = SSM_T
    ns = SLAB_STATE
    out_w = t_steps * LANES
    nt = (((1,), (1,)), ((), ()))
    hp = lax.Precision.HIGHEST
    lr = lrc_ref[...]
    li = lic_ref[...]
    a_re, a_im = _discretise(lr, li, ldc_ref[...])
    den = lr * lr + li * li
    nr = a_re - 1.0
    coef_re = (nr * lr + a_im * li) / den
    coef_im = (a_im * lr - nr * li) / den
    bb_re, bb_im = _cmul(coef_re, coef_im, br_ref[...], bi_ref[...])
    row_g = lax.broadcasted_iota(jnp.int32, lr.shape, 0) // SSM_GROUP
    col_g = lax.broadcasted_iota(jnp.int32, lr.shape, 1) // SSM_STATE
    same_state = row_g == col_g
    row_k = lax.broadcasted_iota(jnp.int32, (LANES, LANES), 0) // SSM_GROUP
    col_k = lax.broadcasted_iota(jnp.int32, (LANES, LANES), 1) // SSM_GROUP
    same_lag = row_k == col_k
    c_re = cr_ref[...]
    c_im = ci_ref[...]
    a64_re = a_re[:, :SSM_STATE]
    a64_im = a_im[:, :SSM_STATE]
    bb64_re = bb_re[:, :SSM_STATE]
    bb64_im = bb_im[:, :SSM_STATE]
    p_re = jnp.ones_like(lr)
    p_im = jnp.zeros_like(lr)
    q_re = jnp.ones_like(a64_re)
    q_im = jnp.zeros_like(a64_re)
    lag_blocks = []
    for d in range(t_steps):
        s = t_steps - 1 - d
        bs_re, bs_im = _cmul(bb_re, bb_im, p_re, p_im)
        w_ref[s, :, out_w:out_w + ns] = jnp.where(same_state, bs_re, 0.0).astype(w_ref.dtype)
        w_ref[s, :, out_w + ns:] = jnp.where(same_state, bs_im, 0.0).astype(w_ref.dtype)
        cd_re, cd_im = _cmul(c_re, c_im, q_re, q_im)
        kd = (lax.dot_general(bb64_re, cd_re, nt, precision=hp, preferred_element_type=F32)
              - lax.dot_general(bb64_im, cd_im, nt, precision=hp, preferred_element_type=F32))
        lag_blocks.append(jnp.where(same_lag, kd, 0.0).astype(w_ref.dtype))
        p_re, p_im = _cmul(p_re, p_im, a_re, a_im)
        q_re, q_im = _cmul(q_re, q_im, a64_re, a64_im)
    zero_block = jnp.zeros((LANES, LANES), w_ref.dtype)
    for s in range(t_steps):
        for t in range(t_steps):
            w_ref[s, :, t * LANES:(t + 1) * LANES] = lag_blocks[t - s] if t >= s else zero_block
    ap_re, ap_im = _discretise(lrp_ref[...], lip_ref[...], ldp_ref[...])
    row_p = lax.broadcasted_iota(jnp.int32, ap_re.shape, 0) // SSM_STATE
    col_p = lax.broadcasted_iota(jnp.int32, ap_re.shape, 1) // SSM_GROUP
    same_out = row_p == col_p
    ct_re = ctr_ref[...]
    ct_im = cti_ref[...]
    r_re, r_im = ap_re, ap_im
    for t in range(t_steps):
        cd_re, cd_im = _cmul(ct_re, ct_im, r_re, r_im)
        cols = slice(t * LANES, (t + 1) * LANES)
        cs_ref[:ns, cols] = jnp.where(same_out, cd_re, 0.0).astype(cs_ref.dtype)
        cs_ref[ns:, cols] = jnp.where(same_out, -cd_im, 0.0).astype(cs_ref.dtype)
        r_re, r_im = _cmul(r_re, r_im, ap_re, ap_im)
    al_re, al_im = _discretise(lrl_ref[...], lil_ref[...], ldl_ref[...])
    t_re, t_im = al_re, al_im
    for _ in range(t_steps - 1):
        t_re, t_im = _cmul(t_re, t_im, al_re, al_im)
    at_ref[:, :ns] = t_re
    at_ref[:, ns:] = t_im


def _ssm_prep(lam_re, lam_im, log_dt, b_re, b_im, c_re, c_im, d_skip):
    g, p = lam_re.shape
    gs = GROUPS_PER_SLAB
    rows = g * SSM_GROUP
    slabs = rows // LANES
    ns = SLAB_STATE
    width = SSM_T * LANES
    ldt = jnp.broadcast_to(log_dt.reshape(g, 1), (g, p))
    chan = lambda a: jnp.tile(jnp.repeat(a, SSM_GROUP, axis=0), (1, gs))
    b_t = lambda a: jnp.tile(jnp.transpose(a, (0, 2, 1)).reshape(rows, p), (1, gs))
    stat = lambda a: jnp.broadcast_to(a.reshape(g * p, 1), (g * p, LANES))
    c_t = lambda a: jnp.tile(jnp.transpose(a, (0, 2, 1)).reshape(g * p, SSM_GROUP), (1, gs))
    lane = lambda a: a.reshape(slabs, 1, ns)
    chan_spec = pl.BlockSpec((LANES, ns), lambda j: (j, 0))
    stat_spec = pl.BlockSpec((ns, LANES), lambda j: (j, 0))
    lane_spec = pl.BlockSpec((None, 1, ns), lambda j: (j, 0, 0))
    w_in, cs, a_t = pl.pallas_call(
        _ssm_prep_kernel,
        grid=(slabs,),
        in_specs=[chan_spec] * 5 + [pl.BlockSpec((LANES, p), lambda j: (j, 0))] * 2
                 + [stat_spec] * 5 + [lane_spec] * 3,
        out_specs=[
            pl.BlockSpec((SSM_T, LANES, width + 2 * ns), lambda j: (0, j, 0)),
            pl.BlockSpec((None, 2 * ns, width), lambda j: (j, 0, 0)),
            pl.BlockSpec((None, 1, 2 * ns), lambda j: (j, 0, 0)),
        ],
        out_shape=[
            jax.ShapeDtypeStruct((SSM_T, rows, width + 2 * ns), BF16),
            jax.ShapeDtypeStruct((slabs, 2 * ns, width), BF16),
            jax.ShapeDtypeStruct((slabs, 1, 2 * ns), F32),
        ],
        compiler_params=_params(("parallel",)),
        name="ssm_prep",
    )(chan(lam_re), chan(lam_im), chan(ldt), b_t(b_re), b_t(b_im),
      c_re.reshape(rows, p), c_im.reshape(rows, p),
      stat(lam_re), stat(lam_im), stat(ldt), c_t(c_re), c_t(c_im),
      lane(lam_re), lane(lam_im), lane(ldt))
    dsk = jnp.tile(d_skip.reshape(slabs, 1, LANES), (1, 1, SSM_T))
    return w_in, cs, a_t, dsk


def _ssm_kernel(u_ref, w_ref, cs_ref, at_ref, dsk_ref, y_ref, s_scr, hp_scr, h_scr,
                *, batch, n_chunks):
    ns = SLAB_STATE
    out_w = SSM_T * LANES

    @pl.when(pl.program_id(1) == 0)
    def _():
        h_scr[...] = jnp.zeros_like(h_scr)

    u = jnp.concatenate([u_ref[:, s, :] for s in range(SSM_T)], axis=-1)
    w = w_ref[...].reshape(out_w, out_w + 2 * ns)
    r = jnp.dot(u.astype(BF16), w, preferred_element_type=F32)
    s_scr[...] = r[:, out_w:]
    a_re = jnp.broadcast_to(at_ref[:, :ns], (batch, ns))
    a_im = jnp.broadcast_to(at_ref[:, ns:], (batch, ns))

    sub = 8
    per_tile = sub // batch

    def step(k, carry):
        h_re, h_im = carry
        r0 = pl.multiple_of(k * sub, sub)
        s_re = s_scr[pl.ds(r0, sub), :ns]
        s_im = s_scr[pl.ds(r0, sub), ns:]
        prev_re, prev_im = [], []
        for c in range(per_tile):
            rows = slice(c * batch, (c + 1) * batch)
            prev_re.append(h_re)
            prev_im.append(h_im)
            h_re, h_im = (a_re * h_re - a_im * h_im + s_re[rows],
                          a_re * h_im + a_im * h_re + s_im[rows])
        hp_scr[pl.ds(r0, sub), :ns] = jnp.concatenate(prev_re, axis=0)
        hp_scr[pl.ds(r0, sub), ns:] = jnp.concatenate(prev_im, axis=0)
        return h_re, h_im

    h_re, h_im = lax.fori_loop(0, n_chunks // per_tile, step,
                               (h_scr[:, :ns], h_scr[:, ns:]))
    h_scr[:, :ns] = h_re
    h_scr[:, ns:] = h_im
    y = (r[:, :out_w]
         + jnp.dot(hp_scr[...].astype(BF16), cs_ref[...], preferred_element_type=F32)
         + dsk_ref[...] * u)
    for t in range(SSM_T):
        y_ref[:, t, :] = y[:, t * LANES:(t + 1) * LANES]


def _ssm(u5, w_in, cs, a_t, dsk, n_rows=512):
    lc, b, slabs, t, _ = u5.shape
    rows = lc * b
    width = t * LANES
    u4 = u5.reshape(rows, slabs, t, LANES)
    n_rows = min(n_rows, rows)
    kern = functools.partial(_ssm_kernel, batch=b, n_chunks=n_rows // b)
    y4 = pl.pallas_call(
        kern,
        grid=(slabs, rows // n_rows),
        in_specs=[
            pl.BlockSpec((n_rows, None, t, LANES), lambda j, i: (i, j, 0, 0)),
            pl.BlockSpec((t, LANES, width + 2 * SLAB_STATE), lambda j, i: (0, j, 0)),
            pl.BlockSpec((None, 2 * SLAB_STATE, width), lambda j, i: (j, 0, 0)),
            pl.BlockSpec((None, 1, 2 * SLAB_STATE), lambda j, i: (j, 0, 0)),
            pl.BlockSpec((None, 1, width), lambda j, i: (j, 0, 0)),
        ],
        out_specs=pl.BlockSpec((n_rows, None, t, LANES), lambda j, i: (i, j, 0, 0)),
        out_shape=jax.ShapeDtypeStruct((rows, slabs, t, LANES), F32),
        scratch_shapes=[
            pltpu.VMEM((n_rows, 2 * SLAB_STATE), F32),
            pltpu.VMEM((n_rows, 2 * SLAB_STATE), F32),
            pltpu.VMEM((b, 2 * SLAB_STATE), F32),
        ],
        compiler_params=_params(("parallel", "arbitrary")),
        name="ssm_scan",
    )(u4, w_in, cs, a_t, dsk)
    return y4.reshape(lc, b, slabs, t, LANES)


def _glu_out_kernel(y_ref, z_ref, wg_ref, bg_ref, wo_ref, x_ref, gate_ref, g_ref, o_ref):
    slabs = y_ref.shape[1]
    tm = z_ref.shape[0]
    y = jnp.concatenate([y_ref[:, j].reshape(tm, LANES) for j in range(slabs)], axis=-1)
    g = jax.nn.gelu(y)
    t = jnp.dot(g.astype(BF16), wg_ref[...], preferred_element_type=F32) + bg_ref[...]
    y2 = g * jax.nn.sigmoid(t) * jax.nn.silu(z_ref[...].astype(F32))
    o = jnp.dot(y2.astype(BF16), wo_ref[...], preferred_element_type=F32)
    o_ref[...] = x_ref[...] + gate_ref[...] * _rms(o, g_ref[...])


def _glu_out(y5, z, w_glu, b_glu, w_out, x, gate, g_post, tm=512):
    b, l, d = x.shape
    _, _, slabs, t, _ = y5.shape
    width = slabs * LANES
    return pl.pallas_call(
        _glu_out_kernel,
        grid=(b, l // tm),
        in_specs=[
            pl.BlockSpec((tm // t, None, slabs, t, LANES), lambda bi, i: (i, bi, 0, 0, 0)),
            pl.BlockSpec((None, tm, width), lambda bi, i: (bi, i, 0)),
            pl.BlockSpec((width, width), lambda bi, i: (0, 0)),
            pl.BlockSpec((1, width), lambda bi, i: (0, 0)),
            pl.BlockSpec((width, d), lambda bi, i: (0, 0)),
            pl.BlockSpec((None, tm, d), lambda bi, i: (bi, i, 0)),
            pl.BlockSpec((None, 1, d), lambda bi, i: (bi, 0, 0)),
            pl.BlockSpec((1, d), lambda bi, i: (0, 0)),
        ],
        out_specs=pl.BlockSpec((None, tm, d), lambda bi, i: (bi, i, 0)),
        out_shape=jax.ShapeDtypeStruct((b, l, d), F32),
        compiler_params=_params(("parallel", "parallel")),
        name="glu_out",
    )(y5, z, w_glu, b_glu.reshape(1, width), w_out, x, gate.reshape(b, 1, d),
      g_post.reshape(1, d))


def _ab_layer(x, mod, g_pre, g_post, w_in, w_out, sgu_g, sgu_w, sgu_b):
    d = x.shape[-1]
    shift, scale, gate = mod[:, :d], mod[:, d:2 * d], mod[:, 2 * d:]
    wa = d // 2
    heads = wa // HEAD_DIM
    proj = _inproj(x, g_pre, scale, shift, w_in.astype(BF16))
    out_a = _sgu(proj, sgu_w, sgu_b, sgu_g, wa)
    attn_heads = 4
    blk = lambda cols: cols // (attn_heads * HEAD_DIM)
    out_b = _attention(proj, blk(3 * wa), blk(4 * wa), blk(5 * wa), blk(6 * wa), heads,
                       heads=attn_heads)
    return _ab_out(out_a, out_b, w_out.astype(BF16), x, gate, g_post)


def _ssm_layer(x, mod, g_pre, g_post, w_in, w_out, lam_re, lam_im, b_re, b_im, c_re, c_im,
               d_skip, log_dt, w_glu, b_glu):
    d = x.shape[-1]
    shift, scale, gate = mod[:, :d], mod[:, d:2 * d], mod[:, 2 * d:]
    u5, z = _inproj_ssm(x, g_pre, scale, shift, w_in.astype(BF16))
    y5 = _ssm(u5, *_ssm_prep(lam_re, lam_im, log_dt, b_re, b_im, c_re, c_im, d_skip))
    return _glu_out(y5, z, w_glu.astype(BF16), b_glu, w_out.astype(BF16), x, gate, g_post)


def kernel(x, c, ln_pre_g, ln_post_g, w_mod, b_mod, w_in_ab, w_out_ab, sgu_norm_g, sgu_w, sgu_b,
           w_in_ssm, w_out_ssm, lam_re, lam_im, b_re, b_im, c_re, c_im, d_skip, log_dt,
           w_glu, b_glu):
    depth = w_mod.shape[0]
    batch = x.shape[0]
    pad = (-batch) % 8
    c_pad = jnp.pad(c, ((0, pad), (0, 0)))
    mod = _adaln_mod(c_pad, w_mod, b_mod)[:, :batch]
    for layer in range(depth):
        i = layer // 2
        if layer % 2 == 0:
            x = _ab_layer(x, mod[layer], ln_pre_g[layer], ln_post_g[layer], w_in_ab[i],
                          w_out_ab[i], sgu_norm_g[i], sgu_w[i], sgu_b[i])
        else:
            x = _ssm_layer(x, mod[layer], ln_pre_g[layer], ln_post_g[layer], w_in_ssm[i],
                           w_out_ssm[i], lam_re[i], lam_im[i], b_re[i], b_im[i], c_re[i],
                           c_im[i], d_skip[i], log_dt[i], w_glu[i], b_glu[i])
    return x
```

```python
import functools
import math

import jax
import jax.numpy as jnp
from jax import lax
from jax.experimental import pallas as pl
from jax.experimental.pallas import tpu as pltpu

F32 = jnp.float32
BF16 = jnp.bfloat16

EPS = 1e-6
LANES = 128
HEAD_DIM = 128
SGU_CHUNK = 128
SSM_GROUP = 16
SSM_STATE = 64
SSM_T = 8
GROUPS_PER_SLAB = LANES // SSM_GROUP
SLAB_STATE = GROUPS_PER_SLAB * SSM_STATE
VMEM_LIMIT = 56 * 1024 * 1024
EXP_UNDERFLOW = -104.0


def _params(sem, vmem=VMEM_LIMIT):
    return pltpu.CompilerParams(dimension_semantics=sem, vmem_limit_bytes=vmem)


def _rms(y, g):
    ms = jnp.mean(y * y, axis=-1, keepdims=True)
    return y * lax.rsqrt(ms + EPS) * g


def _mod_kernel(c_ref, w_ref, b_ref, o_ref):
    cond = jax.nn.silu(c_ref[...])
    o_ref[...] = jnp.dot(cond.astype(BF16), w_ref[...].astype(BF16),
                         preferred_element_type=F32) + b_ref[...]


def _adaln_mod(c_pad, w_mod, b_mod, tn=512):
    depth, d, n = w_mod.shape
    rows = c_pad.shape[0]
    return pl.pallas_call(
        _mod_kernel,
        grid=(depth, n // tn),
        in_specs=[
            pl.BlockSpec((rows, d), lambda l, j: (0, 0)),
            pl.BlockSpec((None, d, tn), lambda l, j: (l, 0, j)),
            pl.BlockSpec((None, 1, tn), lambda l, j: (l, 0, j)),
        ],
        out_specs=pl.BlockSpec((None, rows, tn), lambda l, j: (l, 0, j)),
        out_shape=jax.ShapeDtypeStruct((depth, rows, n), F32),
        compiler_params=_params(("parallel", "parallel")),
        name="adaln_mod",
    )(c_pad, w_mod, b_mod.reshape(depth, 1, n))


def _prenorm(x_ref, g_ref, sc_ref, sh_ref):
    y = _rms(x_ref[...], g_ref[...])
    return (y * (1.0 + sc_ref[...]) + sh_ref[...]).astype(BF16)


def _inproj_kernel(x_ref, g_ref, sc_ref, sh_ref, w_ref, o_ref, h_scr):
    @pl.when(pl.program_id(2) == 0)
    def _():
        h_scr[...] = _prenorm(x_ref, g_ref, sc_ref, sh_ref)

    o_ref[...] = jnp.dot(h_scr[...], w_ref[...],
                         preferred_element_type=F32).astype(o_ref.dtype)


def _inproj(x, g, scale, shift, w, tm=1024, tn=1024):
    b, l, d = x.shape
    n = w.shape[1]
    tn = math.gcd(n, tn)
    return pl.pallas_call(
        _inproj_kernel,
        grid=(b, l // tm, n // tn),
        in_specs=[
            pl.BlockSpec((None, tm, d), lambda bi, i, j: (bi, i, 0)),
            pl.BlockSpec((1, d), lambda bi, i, j: (0, 0)),
            pl.BlockSpec((None, 1, d), lambda bi, i, j: (bi, 0, 0)),
            pl.BlockSpec((None, 1, d), lambda bi, i, j: (bi, 0, 0)),
            pl.BlockSpec((d, tn), lambda bi, i, j: (0, j)),
        ],
        out_specs=pl.BlockSpec((None, tm, tn), lambda bi, i, j: (bi, i, j)),
        out_shape=jax.ShapeDtypeStruct((b, l, n), BF16),
        scratch_shapes=[pltpu.VMEM((tm, d), BF16)],
        compiler_params=_params(("parallel", "parallel", "arbitrary")),
        name="inproj_ab",
    )(x, g.reshape(1, d), scale.reshape(b, 1, d), shift.reshape(b, 1, d), w)


def _inproj_ssm_kernel(x_ref, g_ref, sc_ref, sh_ref, w_ref, u_ref, z_ref, *, width):
    h = _prenorm(x_ref, g_ref, sc_ref, sh_ref)
    acc = jnp.dot(h, w_ref[...], preferred_element_type=F32)
    tm = acc.shape[0]
    for j in range(width // LANES):
        u_ref[:, j] = acc[:, j * LANES:(j + 1) * LANES].reshape(tm // SSM_T, SSM_T, LANES)
    z_ref[...] = acc[:, width:].astype(z_ref.dtype)


def _inproj_ssm(x, g, scale, shift, w, tm=512):
    b, l, d = x.shape
    width = w.shape[1] // 2
    slabs = width // LANES
    kern = functools.partial(_inproj_ssm_kernel, width=width)
    return pl.pallas_call(
        kern,
        grid=(b, l // tm),
        in_specs=[
            pl.BlockSpec((None, tm, d), lambda bi, i: (bi, i, 0)),
            pl.BlockSpec((1, d), lambda bi, i: (0, 0)),
            pl.BlockSpec((None, 1, d), lambda bi, i: (bi, 0, 0)),
            pl.BlockSpec((None, 1, d), lambda bi, i: (bi, 0, 0)),
            pl.BlockSpec((d, 2 * width), lambda bi, i: (0, 0)),
        ],
        out_specs=[
            pl.BlockSpec((tm // SSM_T, None, slabs, SSM_T, LANES),
                         lambda bi, i: (i, bi, 0, 0, 0)),
            pl.BlockSpec((None, tm, width), lambda bi, i: (bi, i, 0)),
        ],
        out_shape=[
            jax.ShapeDtypeStruct((l // SSM_T, b, slabs, SSM_T, LANES), F32),
            jax.ShapeDtypeStruct((b, l, width), BF16),
        ],
        compiler_params=_params(("parallel", "parallel")),
        name="inproj_ssm",
    )(x, g.reshape(1, d), scale.reshape(b, 1, d), shift.reshape(b, 1, d), w)


def _sgu_kernel(u_ref, v_ref, z_ref, w_ref, b_ref, g_ref, o_ref, *, heads, chunks):
    row = lax.broadcasted_iota(jnp.int32, (SGU_CHUNK, SGU_CHUNK), 0)
    col = lax.broadcasted_iota(jnp.int32, (SGU_CHUNK, SGU_CHUNK), 1)
    causal = col <= row
    for h in range(heads):
        w = jnp.where(causal, w_ref[h], 0.0).astype(BF16)
        bias = b_ref[h]
        gain = g_ref[:, h * HEAD_DIM:(h + 1) * HEAD_DIM]
        for c in range(chunks):
            rs = slice(c * SGU_CHUNK, (c + 1) * SGU_CHUNK)
            cs = slice(h * HEAD_DIM, (h + 1) * HEAD_DIM)
            v = jax.nn.gelu(v_ref[rs, cs].astype(F32))
            vn = _rms(v, gain)
            s = jnp.dot(w, vn.astype(BF16), preferred_element_type=F32) + bias
            u = jax.nn.gelu(u_ref[rs, cs].astype(F32))
            z = jax.nn.silu(z_ref[rs, cs].astype(F32))
            o_ref[rs, cs] = (u * s * z).astype(o_ref.dtype)


def _sgu(proj, sgu_w, sgu_b, sgu_g, width, tm=512):
    b, l, _ = proj.shape
    heads = width // HEAD_DIM

    def col_spec(k):
        return pl.BlockSpec((None, tm, width), lambda bi, i: (bi, i, k))

    kern = functools.partial(_sgu_kernel, heads=heads, chunks=tm // SGU_CHUNK)
    return pl.pallas_call(
        kern,
        grid=(b, l // tm),
        in_specs=[
            col_spec(0), col_spec(1), col_spec(2),
            pl.BlockSpec((heads, SGU_CHUNK, SGU_CHUNK), lambda bi, i: (0, 0, 0)),
            pl.BlockSpec((heads, SGU_CHUNK, 1), lambda bi, i: (0, 0, 0)),
            pl.BlockSpec((1, width), lambda bi, i: (0, 0)),
        ],
        out_specs=pl.BlockSpec((None, tm, width), lambda bi, i: (bi, i, 0)),
        out_shape=jax.ShapeDtypeStruct((b, l, width), BF16),
        compiler_params=_params(("parallel", "parallel")),
        name="sgu",
    )(proj, proj, proj, sgu_w, sgu_b.reshape(heads, SGU_CHUNK, 1), sgu_g.reshape(1, width))


def _attn_kernel(q_ref, k_ref, v_ref, bz_ref, o_ref, acc_scr, car_scr,
                 *, tq, tk, sub, heads, scale):
    qi = pl.program_id(2)
    q0 = qi * tq
    r_i = lax.broadcasted_iota(jnp.int32, (sub, sub), 0)
    c_i = lax.broadcasted_iota(jnp.int32, (sub, sub), 1)
    later_mat = (r_i > c_i).astype(BF16)
    q_pos = q0 + lax.broadcasted_iota(jnp.int32, (tq, tk), 0)
    k_off = lax.broadcasted_iota(jnp.int32, (tq, tk), 1)

    acc_scr[...] = jnp.zeros_like(acc_scr)
    car_scr[...] = jnp.zeros_like(car_scr)

    def block(h, ks, width, limit):
        hc = slice(h * HEAD_DIM, (h + 1) * HEAD_DIM)
        ks = pl.multiple_of(ks, sub)
        q = q_ref[:, hc]
        k = k_ref[pl.ds(ks, width), hc]
        v = v_ref[pl.ds(ks, width), hc]
        z = lax.dot_general(q, k, (((1,), (1,)), ((), ())),
                            preferred_element_type=F32) * scale
        soft = jnp.log(1.0 + jnp.exp(-jnp.abs(z)))
        log_beta = jnp.minimum(z, 0.0) - soft
        log_keep = log_beta - z
        if limit is not None:
            mask = (ks + k_off) < limit
            log_keep = jnp.where(mask, log_keep, 0.0)
        carry = car_scr[h]
        n_sub = width // sub
        ws = [None] * n_sub
        for i in reversed(range(n_sub)):
            cols = slice(i * sub, (i + 1) * sub)
            lk = log_keep[:, cols]
            later = jnp.dot(lk.astype(BF16), later_mat, preferred_element_type=F32)
            ws[i] = jnp.exp(log_beta[:, cols] + later + carry)
            carry = carry + later[:, :1] + lk[:, :1]
        w = jnp.concatenate(ws, axis=1)
        if limit is not None:
            w = jnp.where(mask, w, 0.0)
        acc_scr[h] += jnp.dot(w.astype(BF16), v, preferred_element_type=F32)
        car_scr[h] = carry

    ks0 = jnp.maximum(q0 + tq - tk, 0)
    for h in range(heads):
        block(h, ks0, tk, q_pos)
    n_more = ks0 // sub

    for h in range(heads):
        def live(h=h):
            return jnp.max(car_scr[h]) > EXP_UNDERFLOW

        def cond(state):
            it, alive = state
            return jnp.logical_and(it < n_more, alive)

        def body(state, h=h, live=live):
            it, _ = state
            block(h, ks0 - (it + 1) * sub, sub, None)
            return it + 1, live()

        lax.while_loop(cond, body, (jnp.int32(0), live()))

    gate = jax.nn.silu(bz_ref[...].astype(F32))
    acc = jnp.concatenate([acc_scr[h] for h in range(heads)], axis=-1)
    o_ref[...] = (acc * gate).astype(o_ref.dtype)


def _attention(proj, q_col, k_col, v_col, z_col, n_heads, tq=256, tk=512, sub=256, heads=2):
    b, l, _ = proj.shape
    hw = heads * HEAD_DIM
    kern = functools.partial(_attn_kernel, tq=tq, tk=tk, sub=sub, heads=heads,
                             scale=1.0 / math.sqrt(HEAD_DIM))
    return pl.pallas_call(
        kern,
        grid=(b, n_heads // heads, l // tq),
        in_specs=[
            pl.BlockSpec((None, tq, hw), lambda bi, h, i: (bi, i, q_col + h)),
            pl.BlockSpec((None, l, hw), lambda bi, h, i: (bi, 0, k_col + h)),
            pl.BlockSpec((None, l, hw), lambda bi, h, i: (bi, 0, v_col + h)),
            pl.BlockSpec((None, tq, hw), lambda bi, h, i: (bi, i, z_col + h)),
        ],
        out_specs=pl.BlockSpec((None, tq, hw), lambda bi, h, i: (bi, i, h)),
        out_shape=jax.ShapeDtypeStruct((b, l, n_heads * HEAD_DIM), BF16),
        scratch_shapes=[pltpu.VMEM((heads, tq, HEAD_DIM), F32), pltpu.VMEM((heads, tq, 1), F32)],
        compiler_params=_params(("parallel", "parallel", "parallel")),
        name="stickbreak_attn",
    )(proj, proj, proj, proj)


def _ab_out_kernel(a_ref, b_ref, wa_ref, wb_ref, x_ref, gate_ref, g_ref, o_ref):
    y = (jnp.dot(a_ref[...], wa_ref[...], preferred_element_type=F32)
         + jnp.dot(b_ref[...], wb_ref[...], preferred_element_type=F32))
    o_ref[...] = x_ref[...] + gate_ref[...] * _rms(y, g_ref[...])


def _ab_out(out_a, out_b, w_out, x, gate, g_post, tm=512):
    b, l, d = x.shape
    wa = out_a.shape[-1]
    wb = out_b.shape[-1]
    assert wa == wb
    return pl.pallas_call(
        _ab_out_kernel,
        grid=(b, l // tm),
        in_specs=[
            pl.BlockSpec((None, tm, wa), lambda bi, i: (bi, i, 0)),
            pl.BlockSpec((None, tm, wb), lambda bi, i: (bi, i, 0)),
            pl.BlockSpec((wa, d), lambda bi, i: (0, 0)),
            pl.BlockSpec((wb, d), lambda bi, i: (1, 0)),
            pl.BlockSpec((None, tm, d), lambda bi, i: (bi, i, 0)),
            pl.BlockSpec((None, 1, d), lambda bi, i: (bi, 0, 0)),
            pl.BlockSpec((1, d), lambda bi, i: (0, 0)),
        ],
        out_specs=pl.BlockSpec((None, tm, d), lambda bi, i: (bi, i, 0)),
        out_shape=jax.ShapeDtypeStruct((b, l, d), F32),
        compiler_params=_params(("parallel", "parallel")),
        name="ab_out",
    )(out_a, out_b, w_out, w_out, x, gate.reshape(b, 1, d), g_post.reshape(1, d))


def _discretise(lr, li, ldt):
    dt = jnp.exp(ldt)
    mag = jnp.exp(lr * dt)
    return mag * jnp.cos(li * dt), mag * jnp.sin(li * dt)


def _cmul(x_re, x_im, y_re, y_im):
    return x_re * y_re - x_im * y_im, x_re * y_im + x_im * y_re


def _ssm_prep_kernel(lrc_ref, lic_ref, ldc_ref, br_ref, bi_ref, cr_ref, ci_ref,
                     lrp_ref, lip_ref, ldp_ref, ctr_ref, cti_ref,
                     lrl_ref, lil_ref, ldl_ref,
                     w_ref, cs_ref, at_ref):
    t_steps = SSM_T
    ns = SLAB_STATE
    out_w = t_steps * LANES
    nt = (((1,), (1,)), ((), ()))
    hp = lax.Precision.HIGHEST
    lr = lrc_ref[...]
    li = lic_ref[...]
    a_re, a_im = _discretise(lr, li, ldc_ref[...])
    den = lr * lr + li * li
    nr = a_re - 1.0
    coef_re = (nr * lr + a_im * li) / den
    coef_im = (a_im * lr - nr * li) / den
    bb_re, bb_im = _cmul(coef_re, coef_im, br_ref[...], bi_ref[...])
    row_g = lax.broadcasted_iota(jnp.int32, lr.shape, 0) // SSM_GROUP
    col_g = lax.broadcasted_iota(jnp.int32, lr.shape, 1) // SSM_STATE
    same_state = row_g == col_g
    row_k = lax.broadcasted_iota(jnp.int32, (LANES, LANES), 0) // SSM_GROUP
    col_k = lax.broadcasted_iota(jnp.int32, (LANES, LANES), 1) // SSM_GROUP
    same_lag = row_k == col_k
    c_re = cr_ref[...]
    c_im = ci_ref[...]
    a64_re = a_re[:, :SSM_STATE]
    a64_im = a_im[:, :SSM_STATE]
    bb64_re = bb_re[:, :SSM_STATE]
    bb64_im = bb_im[:, :SSM_STATE]
    p_re = jnp.ones_like(lr)
    p_im = jnp.zeros_like(lr)
    q_re = jnp.ones_like(a64_re)
    q_im = jnp.zeros_like(a64_re)
    lag_blocks = []
    for d in range(t_steps):
        s = t_steps - 1 - d
        bs_re, bs_im = _cmul(bb_re, bb_im, p_re, p_im)
        w_ref[s, :, out_w:out_w + ns] = jnp.where(same_state, bs_re, 0.0).astype(w_ref.dtype)
        w_ref[s, :, out_w + ns:] = jnp.where(same_state, bs_im, 0.0).astype(w_ref.dtype)
        cd_re, cd_im = _cmul(c_re, c_im, q_re, q_im)
        kd = (lax.dot_general(bb64_re, cd_re, nt, precision=hp, preferred_element_type=F32)
              - lax.dot_general(bb64_im, cd_im, nt, precision=hp, preferred_element_type=F32))
        lag_blocks.append(jnp.where(same_lag, kd, 0.0).astype(w_ref.dtype))
        p_re, p_im = _cmul(p_re, p_im, a_re, a_im)
        q_re, q_im = _cmul(q_re, q_im, a64_re, a64_im)
    zero_block = jnp.zeros((LANES, LANES), w_ref.dtype)
    for s in range(t_steps):
        for t in range(t_steps):
            w_ref[s, :, t * LANES:(t + 1) * LANES] = lag_blocks[t - s] if t >= s else zero_block
    ap_re, ap_im = _discretise(lrp_ref[...], lip_ref[...], ldp_ref[...])
    row_p = lax.broadcasted_iota(jnp.int32, ap_re.shape, 0) // SSM_STATE
    col_p = lax.broadcasted_iota(jnp.int32, ap_re.shape, 1) // SSM_GROUP
    same_out = row_p == col_p
    ct_re = ctr_ref[...]
    ct_im = cti_ref[...]
    r_re, r_im = ap_re, ap_im
    for t in range(t_steps):
        cd_re, cd_im = _cmul(ct_re, ct_im, r_re, r_im)
        cols = slice(t * LANES, (t + 1) * LANES)
        cs_ref[:ns, cols] = jnp.where(same_out, cd_re, 0.0).astype(cs_ref.dtype)
        cs_ref[ns:, cols] = jnp.where(same_out, -cd_im, 0.0).astype(cs_ref.dtype)
        r_re, r_im = _cmul(r_re, r_im, ap_re, ap_im)
    al_re, al_im = _discretise(lrl_ref[...], lil_ref[...], ldl_ref[...])
    t_re, t_im = al_re, al_im
    for _ in range(t_steps - 1):
        t_re, t_im = _cmul(t_re, t_im, al_re, al_im)
    at_ref[:, :ns] = t_re
    at_ref[:, ns:] = t_im


def _ssm_prep(lam_re, lam_im, log_dt, b_re, b_im, c_re, c_im, d_skip):
    g, p = lam_re.shape
    gs = GROUPS_PER_SLAB
    rows = g * SSM_GROUP
    slabs = rows // LANES
    ns = SLAB_STATE
    width = SSM_T * LANES
    ldt = jnp.broadcast_to(log_dt.reshape(g, 1), (g, p))
    chan = lambda a: jnp.tile(jnp.repeat(a, SSM_GROUP, axis=0), (1, gs))
    b_t = lambda a: jnp.tile(jnp.transpose(a, (0, 2, 1)).reshape(rows, p), (1, gs))
    stat = lambda a: jnp.broadcast_to(a.reshape(g * p, 1), (g * p, LANES))
    c_t = lambda a: jnp.tile(jnp.transpose(a, (0, 2, 1)).reshape(g * p, SSM_GROUP), (1, gs))
    lane = lambda a: a.reshape(slabs, 1, ns)
    chan_spec = pl.BlockSpec((LANES, ns), lambda j: (j, 0))
    stat_spec = pl.BlockSpec((ns, LANES), lambda j: (j, 0))
    lane_spec = pl.BlockSpec((None, 1, ns), lambda j: (j, 0, 0))
    w_in, cs, a_t = pl.pallas_call(
        _ssm_prep_kernel,
        grid=(slabs,),
        in_specs=[chan_spec] * 5 + [pl.BlockSpec((LANES, p), lambda j: (j, 0))] * 2
                 + [stat_spec] * 5 + [lane_spec] * 3,
        out_specs=[
            pl.BlockSpec((SSM_T, LANES, width + 2 * ns), lambda j: (0, j, 0)),
            pl.BlockSpec((None, 2 * ns, width), lambda j: (j, 0, 0)),
            pl.BlockSpec((None, 1, 2 * ns), lambda j: (j, 0, 0)),
        ],
        out_shape=[
            jax.ShapeDtypeStruct((SSM_T, rows, width + 2 * ns), BF16),
            jax.ShapeDtypeStruct((slabs, 2 * ns, width), BF16),
            jax.ShapeDtypeStruct((slabs, 1, 2 * ns), F32),
        ],
        compiler_params=_params(("parallel",)),
        name="ssm_prep",
    )(chan(lam_re), chan(lam_im), chan(ldt), b_t(b_re), b_t(b_im),
      c_re.reshape(rows, p), c_im.reshape(rows, p),
      stat(lam_re), stat(lam_im), stat(ldt), c_t(c_re), c_t(c_im),
      lane(lam_re), lane(lam_im), lane(ldt))
    dsk = jnp.tile(d_skip.reshape(slabs, 1, LANES), (1, 1, SSM_T))
    return w_in, cs, a_t, dsk


def _ssm_kernel(u_ref, w_ref, cs_ref, at_ref, dsk_ref, y_ref, s_scr, hp_scr, h_scr,
                *, batch, n_chunks):
    ns = SLAB_STATE
    out_w = SSM_T * LANES

    @pl.when(pl.program_id(1) == 0)
    def _():
        h_scr[...] = jnp.zeros_like(h_scr)

    u = jnp.concatenate([u_ref[:, s, :] for s in range(SSM_T)], axis=-1)
    w = w_ref[...].reshape(out_w, out_w + 2 * ns)
    r = jnp.dot(u.astype(BF16), w, preferred_element_type=F32)
    s_scr[...] = r[:, out_w:]
    a_re = jnp.broadcast_to(at_ref[:, :ns], (batch, ns))
    a_im = jnp.broadcast_to(at_ref[:, ns:], (batch, ns))

    sub = 8
    per_tile = sub // batch

    def step(k, carry):
        h_re, h_im = carry
        r0 = pl.multiple_of(k * sub, sub)
        s_re = s_scr[pl.ds(r0, sub), :ns]
        s_im = s_scr[pl.ds(r0, sub), ns:]
        prev_re, prev_im = [], []
        for c in range(per_tile):
            rows = slice(c * batch, (c + 1) * batch)
            prev_re.append(h_re)
            prev_im.append(h_im)
            h_re, h_im = (a_re * h_re - a_im * h_im + s_re[rows],
                          a_re * h_im + a_im * h_re + s_im[rows])
        hp_scr[pl.ds(r0, sub), :ns] = jnp.concatenate(prev_re, axis=0)
        hp_scr[pl.ds(r0, sub), ns:] = jnp.concatenate(prev_im, axis=0)
        return h_re, h_im

    h_re, h_im = lax.fori_loop(0, n_chunks // per_tile, step,
                               (h_scr[:, :ns], h_scr[:, ns:]))
    h_scr[:, :ns] = h_re
    h_scr[:, ns:] = h_im
    y = (r[:, :out_w]
         + jnp.dot(hp_scr[...].astype(BF16), cs_ref[...], preferred_element_type=F32)
         + dsk_ref[...] * u)
    for t in range(SSM_T):
        y_ref[:, t, :] = y[:, t * LANES:(t + 1) * LANES]


def _ssm(u5, w_in, cs, a_t, dsk, n_rows=512):
    lc, b, slabs, t, _ = u5.shape
    rows = lc * b
    width = t * LANES
    u4 = u5.reshape(rows, slabs, t, LANES)
    n_rows = min(n_rows, rows)
    kern = functools.partial(_ssm_kernel, batch=b, n_chunks=n_rows // b)
    y4 = pl.pallas_call(
        kern,
        grid=(slabs, rows // n_rows),
        in_specs=[
            pl.BlockSpec((n_rows, None, t, LANES), lambda j, i: (i, j, 0, 0)),
            pl.BlockSpec((t, LANES, width + 2 * SLAB_STATE), lambda j, i: (0, j, 0)),
            pl.BlockSpec((None, 2 * SLAB_STATE, width), lambda j, i: (j, 0, 0)),
            pl.BlockSpec((None, 1, 2 * SLAB_STATE), lambda j, i: (j, 0, 0)),
            pl.BlockSpec((None, 1, width), lambda j, i: (j, 0, 0)),
        ],
        out_specs=pl.BlockSpec((n_rows, None, t, LANES), lambda j, i: (i, j, 0, 0)),
        out_shape=jax.ShapeDtypeStruct((rows, slabs, t, LANES), F32),
        scratch_shapes=[
            pltpu.VMEM((n_rows, 2 * SLAB_STATE), F32),
            pltpu.VMEM((n_rows, 2 * SLAB_STATE), F32),
            pltpu.VMEM((b, 2 * SLAB_STATE), F32),
        ],
        compiler_params=_params(("parallel", "arbitrary")),
        name="ssm_scan",
    )(u4, w_in, cs, a_t, dsk)
    return y4.reshape(lc, b, slabs, t, LANES)


def _glu_out_kernel(y_ref, z_ref, wg_ref, bg_ref, wo_ref, x_ref, gate_ref, g_ref, o_ref):
    slabs = y_ref.shape[1]
    tm = z_ref.shape[0]
    y = jnp.concatenate([y_ref[:, j].reshape(tm, LANES) for j in range(slabs)], axis=-1)
    g = jax.nn.gelu(y)
    t = jnp.dot(g.astype(BF16), wg_ref[...], preferred_element_type=F32) + bg_ref[...]
    y2 = g * jax.nn.sigmoid(t) * jax.nn.silu(z_ref[...].astype(F32))
    o = jnp.dot(y2.astype(BF16), wo_ref[...], preferred_element_type=F32)
    o_ref[...] = x_ref[...] + gate_ref[...] * _rms(o, g_ref[...])


def _glu_out(y5, z, w_glu, b_glu, w_out, x, gate, g_post, tm=512):
    b, l, d = x.shape
    _, _, slabs, t, _ = y5.shape
    width = slabs * LANES
    return pl.pallas_call(
        _glu_out_kernel,
        grid=(b, l // tm),
        in_specs=[
            pl.BlockSpec((tm // t, None, slabs, t, LANES), lambda bi, i: (i, bi, 0, 0, 0)),
            pl.BlockSpec((None, tm, width), lambda bi, i: (bi, i, 0)),
            pl.BlockSpec((width, width), lambda bi, i: (0, 0)),
            pl.BlockSpec((1, width), lambda bi, i: (0, 0)),
            pl.BlockSpec((width, d), lambda bi, i: (0, 0)),
            pl.BlockSpec((None, tm, d), lambda bi, i: (bi, i, 0)),
            pl.BlockSpec((None, 1, d), lambda bi, i: (bi, 0, 0)),
            pl.BlockSpec((1, d), lambda bi, i: (0, 0)),
        ],
        out_specs=pl.BlockSpec((None, tm, d), lambda bi, i: (bi, i, 0)),
        out_shape=jax.ShapeDtypeStruct((b, l, d), F32),
        compiler_params=_params(("parallel", "parallel")),
        name="glu_out",
    )(y5, z, w_glu, b_glu.reshape(1, width), w_out, x, gate.reshape(b, 1, d),
      g_post.reshape(1, d))


def _ab_layer(x, mod, g_pre, g_post, w_in, w_out, sgu_g, sgu_w, sgu_b):
    d = x.shape[-1]
    shift, scale, gate = mod[:, :d], mod[:, d:2 * d], mod[:, 2 * d:]
    wa = d // 2
    heads = wa // HEAD_DIM
    proj = _inproj(x, g_pre, scale, shift, w_in.astype(BF16))
    out_a = _sgu(proj, sgu_w, sgu_b, sgu_g, wa)
    attn_heads = 4
    blk = lambda cols: cols // (attn_heads * HEAD_DIM)
    out_b = _attention(proj, blk(3 * wa), blk(4 * wa), blk(5 * wa), blk(6 * wa), heads,
                       heads=attn_heads)
    return _ab_out(out_a, out_b, w_out.astype(BF16), x, gate, g_post)


def _ssm_layer(x, mod, g_pre, g_post, w_in, w_out, lam_re, lam_im, b_re, b_im, c_re, c_im,
               d_skip, log_dt, w_glu, b_glu):
    d = x.shape[-1]
    shift, scale, gate = mod[:, :d], mod[:, d:2 * d], mod[:, 2 * d:]
    u5, z = _inproj_ssm(x, g_pre, scale, shift, w_in.astype(BF16))
    y5 = _ssm(u5, *_ssm_prep(lam_re, lam_im, log_dt, b_re, b_im, c_re, c_im, d_skip))
    return _glu_out(y5, z, w_glu.astype(BF16), b_glu, w_out.astype(BF16), x, gate, g_post)


def kernel(x, c, ln_pre_g, ln_post_g, w_mod, b_mod, w_in_ab, w_out_ab, sgu_norm_g, sgu_w, sgu_b,
           w_in_ssm, w_out_ssm, lam_re, lam_im, b_re, b_im, c_re, c_im, d_skip, log_dt,
           w_glu, b_glu):
    depth = w_mod.shape[0]
    batch = x.shape[0]
    pad = (-batch) % 8
    c_pad = jnp.pad(c, ((0, pad), (0, 0)))
    mod = _adaln_mod(c_pad, w_mod, b_mod)[:, :batch]
    for layer in range(depth):
        i = layer // 2
        if layer % 2 == 0:
            x = _ab_layer(x, mod[layer], ln_pre_g[layer], ln_post_g[layer], w_in_ab[i],
                          w_out_ab[i], sgu_norm_g[i], sgu_w[i], sgu_b[i])
        else:
            x = _ssm_layer(x, mod[layer], ln_pre_g[layer], ln_post_g[layer], w_in_ssm[i],
                           w_out_ssm[i], lam_re[i], lam_im[i], b_re[i], b_im[i], c_re[i],
                           c_im[i], d_skip[i], log_dt[i], w_glu[i], b_glu[i])
    return x
```

```python
import functools
import math

import jax
import jax.numpy as jnp
from jax import lax
from jax.experimental import pallas as pl
from jax.experimental.pallas import tpu as pltpu

F32 = jnp.float32
BF16 = jnp.bfloat16

EPS = 1e-6
LANES = 128
HEAD_DIM = 128
SGU_CHUNK = 128
SSM_GROUP = 16
SSM_STATE = 64
SSM_T = 8
GROUPS_PER_SLAB = LANES // SSM_GROUP
SLAB_STATE = GROUPS_PER_SLAB * SSM_STATE
VMEM_LIMIT = 56 * 1024 * 1024
EXP_UNDERFLOW = -104.0


def _params(sem, vmem=VMEM_LIMIT):
    return pltpu.CompilerParams(dimension_semantics=sem, vmem_limit_bytes=vmem)


def _rms(y, g):
    ms = jnp.mean(y * y, axis=-1, keepdims=True)
    return y * lax.rsqrt(ms + EPS) * g


def _mod_kernel(c_ref, w_ref, b_ref, o_ref):
    cond = jax.nn.silu(c_ref[...])
    o_ref[...] = jnp.dot(cond.astype(BF16), w_ref[...].astype(BF16),
                         preferred_element_type=F32) + b_ref[...]


def _adaln_mod(c_pad, w_mod, b_mod, tn=512):
    depth, d, n = w_mod.shape
    rows = c_pad.shape[0]
    return pl.pallas_call(
        _mod_kernel,
        grid=(depth, n // tn),
        in_specs=[
            pl.BlockSpec((rows, d), lambda l, j: (0, 0)),
            pl.BlockSpec((None, d, tn), lambda l, j: (l, 0, j)),
            pl.BlockSpec((None, 1, tn), lambda l, j: (l, 0, j)),
        ],
        out_specs=pl.BlockSpec((None, rows, tn), lambda l, j: (l, 0, j)),
        out_shape=jax.ShapeDtypeStruct((depth, rows, n), F32),
        compiler_params=_params(("parallel", "parallel")),
        name="adaln_mod",
    )(c_pad, w_mod, b_mod.reshape(depth, 1, n))


def _prenorm(x_ref, g_ref, sc_ref, sh_ref):
    y = _rms(x_ref[...], g_ref[...])
    return (y * (1.0 + sc_ref[...]) + sh_ref[...]).astype(BF16)


def _inproj_kernel(x_ref, g_ref, sc_ref, sh_ref, w_ref, o_ref, h_scr):
    @pl.when(pl.program_id(2) == 0)
    def _():
        h_scr[...] = _prenorm(x_ref, g_ref, sc_ref, sh_ref)

    o_ref[...] = jnp.dot(h_scr[...], w_ref[...],
                         preferred_element_type=F32).astype(o_ref.dtype)


def _inproj(x, g, scale, shift, w, tm=1024, tn=1024):
    b, l, d = x.shape
    n = w.shape[1]
    tn = math.gcd(n, tn)
    return pl.pallas_call(
        _inproj_kernel,
        grid=(b, l // tm, n // tn),
        in_specs=[
            pl.BlockSpec((None, tm, d), lambda bi, i, j: (bi, i, 0)),
            pl.BlockSpec((1, d), lambda bi, i, j: (0, 0)),
            pl.BlockSpec((None, 1, d), lambda bi, i, j: (bi, 0, 0)),
            pl.BlockSpec((None, 1, d), lambda bi, i, j: (bi, 0, 0)),
            pl.BlockSpec((d, tn), lambda bi, i, j: (0, j)),
        ],
        out_specs=pl.BlockSpec((None, tm, tn), lambda bi, i, j: (bi, i, j)),
        out_shape=jax.ShapeDtypeStruct((b, l, n), BF16),
        scratch_shapes=[pltpu.VMEM((tm, d), BF16)],
        compiler_params=_params(("parallel", "parallel", "arbitrary")),
        name="inproj_ab",
    )(x, g.reshape(1, d), scale.reshape(b, 1, d), shift.reshape(b, 1, d), w)


def _inproj_ssm_kernel(x_ref, g_ref, sc_ref, sh_ref, w_ref, u_ref, z_ref, *, width):
    h = _prenorm(x_ref, g_ref, sc_ref, sh_ref)
    acc = jnp.dot(h, w_ref[...], preferred_element_type=F32)
    tm = acc.shape[0]
    for j in range(width // LANES):
        u_ref[:, j] = acc[:, j * LANES:(j + 1) * LANES].reshape(tm // SSM_T, SSM_T, LANES)
    z_ref[...] = acc[:, width:].astype(z_ref.dtype)


def _inproj_ssm(x, g, scale, shift, w, tm=512):
    b, l, d = x.shape
    width = w.shape[1] // 2
    slabs = width // LANES
    kern = functools.partial(_inproj_ssm_kernel, width=width)
    return pl.pallas_call(
        kern,
        grid=(b, l // tm),
        in_specs=[
            pl.BlockSpec((None, tm, d), lambda bi, i: (bi, i, 0)),
            pl.BlockSpec((1, d), lambda bi, i: (0, 0)),
            pl.BlockSpec((None, 1, d), lambda bi, i: (bi, 0, 0)),
            pl.BlockSpec((None, 1, d), lambda bi, i: (bi, 0, 0)),
            pl.BlockSpec((d, 2 * width), lambda bi, i: (0, 0)),
        ],
        out_specs=[
            pl.BlockSpec((tm // SSM_T, None, slabs, SSM_T, LANES),
                         lambda bi, i: (i, bi, 0, 0, 0)),
            pl.BlockSpec((None, tm, width), lambda bi, i: (bi, i, 0)),
        ],
        out_shape=[
            jax.ShapeDtypeStruct((l // SSM_T, b, slabs, SSM_T, LANES), F32),
            jax.ShapeDtypeStruct((b, l, width), BF16),
        ],
        compiler_params=_params(("parallel", "parallel")),
        name="inproj_ssm",
    )(x, g.reshape(1, d), scale.reshape(b, 1, d), shift.reshape(b, 1, d), w)


def _sgu_kernel(u_ref, v_ref, z_ref, w_ref, b_ref, g_ref, o_ref, *, heads, chunks):
    row = lax.broadcasted_iota(jnp.int32, (SGU_CHUNK, SGU_CHUNK), 0)
    col = lax.broadcasted_iota(jnp.int32, (SGU_CHUNK, SGU_CHUNK), 1)
    causal = col <= row
    for h in range(heads):
        w = jnp.where(causal, w_ref[h], 0.0).astype(BF16)
        bias = b_ref[h]
        gain = g_ref[:, h * HEAD_DIM:(h + 1) * HEAD_DIM]
        for c in range(chunks):
            rs = slice(c * SGU_CHUNK, (c + 1) * SGU_CHUNK)
            cs = slice(h * HEAD_DIM, (h + 1) * HEAD_DIM)
            v = jax.nn.gelu(v_ref[rs, cs].astype(F32))
            vn = _rms(v, gain)
            s = jnp.dot(w, vn.astype(BF16), preferred_element_type=F32) + bias
            u = jax.nn.gelu(u_ref[rs, cs].astype(F32))
            z = jax.nn.silu(z_ref[rs, cs].astype(F32))
            o_ref[rs, cs] = (u * s * z).astype(o_ref.dtype)


def _sgu(proj, sgu_w, sgu_b, sgu_g, width, tm=512):
    b, l, _ = proj.shape
    heads = width // HEAD_DIM

    def col_spec(k):
        return pl.BlockSpec((None, tm, width), lambda bi, i: (bi, i, k))

    kern = functools.partial(_sgu_kernel, heads=heads, chunks=tm // SGU_CHUNK)
    return pl.pallas_call(
        kern,
        grid=(b, l // tm),
        in_specs=[
            col_spec(0), col_spec(1), col_spec(2),
            pl.BlockSpec((heads, SGU_CHUNK, SGU_CHUNK), lambda bi, i: (0, 0, 0)),
            pl.BlockSpec((heads, SGU_CHUNK, 1), lambda bi, i: (0, 0, 0)),
            pl.BlockSpec((1, width), lambda bi, i: (0, 0)),
        ],
        out_specs=pl.BlockSpec((None, tm, width), lambda bi, i: (bi, i, 0)),
        out_shape=jax.ShapeDtypeStruct((b, l, width), BF16),
        compiler_params=_params(("parallel", "parallel")),
        name="sgu",
    )(proj, proj, proj, sgu_w, sgu_b.reshape(heads, SGU_CHUNK, 1), sgu_g.reshape(1, width))


def _attn_kernel(q_ref, k_ref, v_ref, bz_ref, o_ref, acc_scr, car_scr,
                 *, tq, tk, sub, heads, scale):
    qi = pl.program_id(2)
    q0 = qi * tq
    r_i = lax.broadcasted_iota(jnp.int32, (sub, sub), 0)
    c_i = lax.broadcasted_iota(jnp.int32, (sub, sub), 1)
    later_mat = (r_i > c_i).astype(BF16)
    q_pos = q0 + lax.broadcasted_iota(jnp.int32, (tq, tk), 0)
    k_off = lax.broadcasted_iota(jnp.int32, (tq, tk), 1)

    acc_scr[...] = jnp.zeros_like(acc_scr)
    car_scr[...] = jnp.zeros_like(car_scr)

    def block(h, ks, width, limit):
        hc = slice(h * HEAD_DIM, (h + 1) * HEAD_DIM)
        ks = pl.multiple_of(ks, sub)
        q = q_ref[:, hc]
        k = k_ref[pl.ds(ks, width), hc]
        v = v_ref[pl.ds(ks, width), hc]
        z = lax.dot_general(q, k, (((1,), (1,)), ((), ())),
                            preferred_element_type=F32) * scale
        soft = jnp.log(1.0 + jnp.exp(-jnp.abs(z)))
        log_beta = jnp.minimum(z, 0.0) - soft
        log_keep = log_beta - z
        if limit is not None:
            mask = (ks + k_off) < limit
            log_keep = jnp.where(mask, log_keep, 0.0)
        carry = car_scr[h]
        n_sub = width // sub
        ws = [None] * n_sub
        for i in reversed(range(n_sub)):
            cols = slice(i * sub, (i + 1) * sub)
            lk = log_keep[:, cols]
            later = jnp.dot(lk.astype(BF16), later_mat, preferred_element_type=F32)
            ws[i] = jnp.exp(log_beta[:, cols] + later + carry)
            carry = carry + later[:, :1] + lk[:, :1]
        w = jnp.concatenate(ws, axis=1)
        if limit is not None:
            w = jnp.where(mask, w, 0.0)
        acc_scr[h] += jnp.dot(w.astype(BF16), v, preferred_element_type=F32)
        car_scr[h] = carry

    ks0 = jnp.maximum(q0 + tq - tk, 0)
    for h in range(heads):
        block(h, ks0, tk, q_pos)
    n_more = ks0 // sub

    for h in range(heads):
        def live(h=h):
            return jnp.max(car_scr[h]) > EXP_UNDERFLOW

        def cond(state):
            it, alive = state
            return jnp.logical_and(it < n_more, alive)

        def body(state, h=h, live=live):
            it, _ = state
            block(h, ks0 - (it + 1) * sub, sub, None)
            return it + 1, live()

        lax.while_loop(cond, body, (jnp.int32(0), live()))

    gate = jax.nn.silu(bz_ref[...].astype(F32))
    acc = jnp.concatenate([acc_scr[h] for h in range(heads)], axis=-1)
    o_ref[...] = (acc * gate).astype(o_ref.dtype)


def _attention(proj, q_col, k_col, v_col, z_col, n_heads, tq=256, tk=512, sub=256, heads=2):
    b, l, _ = proj.shape
    hw = heads * HEAD_DIM
    kern = functools.partial(_attn_kernel, tq=tq, tk=tk, sub=sub, heads=heads,
                             scale=1.0 / math.sqrt(HEAD_DIM))
    return pl.pallas_call(
        kern,
        grid=(b, n_heads // heads, l // tq),
        in_specs=[
            pl.BlockSpec((None, tq, hw), lambda bi, h, i: (bi, i, q_col + h)),
            pl.BlockSpec((None, l, hw), lambda bi, h, i: (bi, 0, k_col + h)),
            pl.BlockSpec((None, l, hw), lambda bi, h, i: (bi, 0, v_col + h)),
            pl.BlockSpec((None, tq, hw), lambda bi, h, i: (bi, i, z_col + h)),
        ],
        out_specs=pl.BlockSpec((None, tq, hw), lambda bi, h, i: (bi, i, h)),
        out_shape=jax.ShapeDtypeStruct((b, l, n_heads * HEAD_DIM), BF16),
        scratch_shapes=[pltpu.VMEM((heads, tq, HEAD_DIM), F32), pltpu.VMEM((heads, tq, 1), F32)],
        compiler_params=_params(("parallel", "parallel", "parallel")),
        name="stickbreak_attn",
    )(proj, proj, proj, proj)


def _ab_out_kernel(a_ref, b_ref, wa_ref, wb_ref, x_ref, gate_ref, g_ref, o_ref):
    y = (jnp.dot(a_ref[...], wa_ref[...], preferred_element_type=F32)
         + jnp.dot(b_ref[...], wb_ref[...], preferred_element_type=F32))
    o_ref[...] = x_ref[...] + gate_ref[...] * _rms(y, g_ref[...])


def _ab_out(out_a, out_b, w_out, x, gate, g_post, tm=512):
    b, l, d = x.shape
    wa = out_a.shape[-1]
    wb = out_b.shape[-1]
    assert wa == wb
    return pl.pallas_call(
        _ab_out_kernel,
        grid=(b, l // tm),
        in_specs=[
            pl.BlockSpec((None, tm, wa), lambda bi, i: (bi, i, 0)),
            pl.BlockSpec((None, tm, wb), lambda bi, i: (bi, i, 0)),
            pl.BlockSpec((wa, d), lambda bi, i: (0, 0)),
            pl.BlockSpec((wb, d), lambda bi, i: (1, 0)),
            pl.BlockSpec((None, tm, d), lambda bi, i: (bi, i, 0)),
            pl.BlockSpec((None, 1, d), lambda bi, i: (bi, 0, 0)),
            pl.BlockSpec((1, d), lambda bi, i: (0, 0)),
        ],
        out_specs=pl.BlockSpec((None, tm, d), lambda bi, i: (bi, i, 0)),
        out_shape=jax.ShapeDtypeStruct((b, l, d), F32),
        compiler_params=_params(("parallel", "parallel")),
        name="ab_out",
    )(out_a, out_b, w_out, w_out, x, gate.reshape(b, 1, d), g_post.reshape(1, d))


def _discretise(lr, li, ldt):
    dt = jnp.exp(ldt)
    mag = jnp.exp(lr * dt)
    return mag * jnp.cos(li * dt), mag * jnp.sin(li * dt)


def _cmul(x_re, x_im, y_re, y_im):
    return x_re * y_re - x_im * y_im, x_re * y_im + x_im * y_re


def _ssm_prep_kernel(lrc_ref, lic_ref, ldc_ref, br_ref, bi_ref, cr_ref, ci_ref,
                     lrp_ref, lip_ref, ldp_ref, ctr_ref, cti_ref,
                     lrl_ref, lil_ref, ldl_ref,
                     w_ref, cs_ref, at_ref):
    t_steps = SSM_T
    ns = SLAB_STATE
    out_w = t_steps * LANES
    nt = (((1,), (1,)), ((), ()))
    hp = lax.Precision.HIGHEST
    lr = lrc_ref[...]
    li = lic_ref[...]
    a_re, a_im = _discretise(lr, li, ldc_ref[...])
    den = lr * lr + li * li
    nr = a_re - 1.0
    coef_re = (nr * lr + a_im * li) / den
    coef_im = (a_im * lr - nr * li) / den
    bb_re, bb_im = _cmul(coef_re, coef_im, br_ref[...], bi_ref[...])
    row_g = lax.broadcasted_iota(jnp.int32, lr.shape, 0) // SSM_GROUP
    col_g = lax.broadcasted_iota(jnp.int32, lr.shape, 1) // SSM_STATE
    same_state = row_g == col_g
    row_k = lax.broadcasted_iota(jnp.int32, (LANES, LANES), 0) // SSM_GROUP
    col_k = lax.broadcasted_iota(jnp.int32, (LANES, LANES), 1) // SSM_GROUP
    same_lag = row_k == col_k
    c_re = cr_ref[...]
    c_im = ci_ref[...]
    a64_re = a_re[:, :SSM_STATE]
    a64_im = a_im[:, :SSM_STATE]
    bb64_re = bb_re[:, :SSM_STATE]
    bb64_im = bb_im[:, :SSM_STATE]
    p_re = jnp.ones_like(lr)
    p_im = jnp.zeros_like(lr)
    q_re = jnp.ones_like(a64_re)
    q_im = jnp.zeros_like(a64_re)
    lag_blocks = []
    for d in range(t_steps):
        s = t_steps - 1 - d
        bs_re, bs_im = _cmul(bb_re, bb_im, p_re, p_im)
        w_ref[s, :, out_w:out_w + ns] = jnp.where(same_state, bs_re, 0.0).astype(w_ref.dtype)
        w_ref[s, :, out_w + ns:] = jnp.where(same_state, bs_im, 0.0).astype(w_ref.dtype)
        cd_re, cd_im = _cmul(c_re, c_im, q_re, q_im)
        kd = (lax.dot_general(bb64_re, cd_re, nt, precision=hp, preferred_element_type=F32)
              - lax.dot_general(bb64_im, cd_im, nt, precision=hp, preferred_element_type=F32))
        lag_blocks.append(jnp.where(same_lag, kd, 0.0).astype(w_ref.dtype))
        p_re, p_im = _cmul(p_re, p_im, a_re, a_im)
        q_re, q_im = _cmul(q_re, q_im, a64_re, a64_im)
    zero_block = jnp.zeros((LANES, LANES), w_ref.dtype)
    for s in range(t_steps):
        for t in range(t_steps):
            w_ref[s, :, t * LANES:(t + 1) * LANES] = lag_blocks[t - s] if t >= s else zero_block
    ap_re, ap_im = _discretise(lrp_ref[...], lip_ref[...], ldp_ref[...])
    row_p = lax.broadcasted_iota(jnp.int32, ap_re.shape, 0) // SSM_STATE
    col_p = lax.broadcasted_iota(jnp.int32, ap_re.shape, 1) // SSM_GROUP
    same_out = row_p == col_p
    ct_re = ctr_ref[...]
    ct_im = cti_ref[...]
    r_re, r_im = ap_re, ap_im
    for t in range(t_steps):
        cd_re, cd_im = _cmul(ct_re, ct_im, r_re, r_im)
        cols = slice(t * LANES, (t + 1) * LANES)
        cs_ref[:ns, cols] = jnp.where(same_out, cd_re, 0.0).astype(cs_ref.dtype)
        cs_ref[ns:, cols] = jnp.where(same_out, -cd_im, 0.0).astype(cs_ref.dtype)
        r_re, r_im = _cmul(r_re, r_im, ap_re, ap_im)
    al_re, al_im = _discretise(lrl_ref[...], lil_ref[...], ldl_ref[...])
    t_re, t_im = al_re, al_im
    for _ in range(t_steps - 1):
        t_re, t_im = _cmul(t_re, t_im, al_re, al_im)
    at_ref[:, :ns] = t_re
    at_ref[:, ns:] = t_im


def _ssm_prep(lam_re, lam_im, log_dt, b_re, b_im, c_re, c_im, d_skip):
    g, p = lam_re.shape
    gs = GROUPS_PER_SLAB
    rows = g * SSM_GROUP
    slabs = rows // LANES
    ns = SLAB_STATE
    width = SSM_T * LANES
    ldt = jnp.broadcast_to(log_dt.reshape(g, 1), (g, p))
    chan = lambda a: jnp.tile(jnp.repeat(a, SSM_GROUP, axis=0), (1, gs))
    b_t = lambda a: jnp.tile(jnp.transpose(a, (0, 2, 1)).reshape(rows, p), (1, gs))
    stat = lambda a: jnp.broadcast_to(a.reshape(g * p, 1), (g * p, LANES))
    c_t = lambda a: jnp.tile(jnp.transpose(a, (0, 2, 1)).reshape(g * p, SSM_GROUP), (1, gs))
    lane = lambda a: a.reshape(slabs, 1, ns)
    chan_spec = pl.BlockSpec((LANES, ns), lambda j: (j, 0))
    stat_spec = pl.BlockSpec((ns, LANES), lambda j: (j, 0))
    lane_spec = pl.BlockSpec((None, 1, ns), lambda j: (j, 0, 0))
    w_in, cs, a_t = pl.pallas_call(
        _ssm_prep_kernel,
        grid=(slabs,),
        in_specs=[chan_spec] * 5 + [pl.BlockSpec((LANES, p), lambda j: (j, 0))] * 2
                 + [stat_spec] * 5 + [lane_spec] * 3,
        out_specs=[
            pl.BlockSpec((SSM_T, LANES, width + 2 * ns), lambda j: (0, j, 0)),
            pl.BlockSpec((None, 2 * ns, width), lambda j: (j, 0, 0)),
            pl.BlockSpec((None, 1, 2 * ns), lambda j: (j, 0, 0)),
        ],
        out_shape=[
            jax.ShapeDtypeStruct((SSM_T, rows, width + 2 * ns), BF16),
            jax.ShapeDtypeStruct((slabs, 2 * ns, width), BF16),
            jax.ShapeDtypeStruct((slabs, 1, 2 * ns), F32),
        ],
        compiler_params=_params(("parallel",)),
        name="ssm_prep",
    )(chan(lam_re), chan(lam_im), chan(ldt), b_t(b_re), b_t(b_im),
      c_re.reshape(rows, p), c_im.reshape(rows, p),
      stat(lam_re), stat(lam_im), stat(ldt), c_t(c_re), c_t(c_im),
      lane(lam_re), lane(lam_im), lane(ldt))
    dsk = jnp.tile(d_skip.reshape(slabs, 1, LANES), (1, 1, SSM_T))
    return w_in, cs, a_t, dsk


def _ssm_kernel(u_ref, w_ref, cs_ref, at_ref, dsk_ref, y_ref, s_scr, hp_scr, h_scr,
                *, batch, n_chunks):
    ns = SLAB_STATE
    out_w = SSM_T * LANES

    @pl.when(pl.program_id(1) == 0)
    def _():
        h_scr[...] = jnp.zeros_like(h_scr)

    u = jnp.concatenate([u_ref[:, s, :] for s in range(SSM_T)], axis=-1)
    w = w_ref[...].reshape(out_w, out_w + 2 * ns)
    r = jnp.dot(u.astype(BF16), w, preferred_element_type=F32)
    s_scr[...] = r[:, out_w:]
    a_re = jnp.broadcast_to(at_ref[:, :ns], (batch, ns))
    a_im = jnp.broadcast_to(at_ref[:, ns:], (batch, ns))

    sub = 8
    per_tile = sub // batch

    def step(k, carry):
        h_re, h_im = carry
        r0 = pl.multiple_of(k * sub, sub)
        s_re = s_scr[pl.ds(r0, sub), :ns]
        s_im = s_scr[pl.ds(r0, sub), ns:]
        prev_re, prev_im = [], []
        for c in range(per_tile):
            rows = slice(c * batch, (c + 1) * batch)
            prev_re.append(h_re)
            prev_im.append(h_im)
            h_re, h_im = (a_re * h_re - a_im * h_im + s_re[rows],
                          a_re * h_im + a_im * h_re + s_im[rows])
        hp_scr[pl.ds(r0, sub), :ns] = jnp.concatenate(prev_re, axis=0)
        hp_scr[pl.ds(r0, sub), ns:] = jnp.concatenate(prev_im, axis=0)
        return h_re, h_im

    h_re, h_im = lax.fori_loop(0, n_chunks // per_tile, step,
                               (h_scr[:, :ns], h_scr[:, ns:]))
    h_scr[:, :ns] = h_re
    h_scr[:, ns:] = h_im
    y = (r[:, :out_w]
         + jnp.dot(hp_scr[...].astype(BF16), cs_ref[...], preferred_element_type=F32)
         + dsk_ref[...] * u)
    for t in range(SSM_T):
        y_ref[:, t, :] = y[:, t * LANES:(t + 1) * LANES]


def _ssm(u5, w_in, cs, a_t, dsk, n_rows=512):
    lc, b, slabs, t, _ = u5.shape
    rows = lc * b
    width = t * LANES
    u4 = u5.reshape(rows, slabs, t, LANES)
    n_rows = min(n_rows, rows)
    kern = functools.partial(_ssm_kernel, batch=b, n_chunks=n_rows // b)
    y4 = pl.pallas_call(
        kern,
        grid=(slabs, rows // n_rows),
        in_specs=[
            pl.BlockSpec((n_rows, None, t, LANES), lambda j, i: (i, j, 0, 0)),
            pl.BlockSpec((t, LANES, width + 2 * SLAB_STATE), lambda j, i: (0, j, 0)),
            pl.BlockSpec((None, 2 * SLAB_STATE, width), lambda j, i: (j, 0, 0)),
            pl.BlockSpec((None, 1, 2 * SLAB_STATE), lambda j, i: (j, 0, 0)),
            pl.BlockSpec((None, 1, width), lambda j, i: (j, 0, 0)),
        ],
        out_specs=pl.BlockSpec((n_rows, None, t, LANES), lambda j, i: (i, j, 0, 0)),
        out_shape=jax.ShapeDtypeStruct((rows, slabs, t, LANES), F32),
        scratch_shapes=[
            pltpu.VMEM((n_rows, 2 * SLAB_STATE), F32),
            pltpu.VMEM((n_rows, 2 * SLAB_STATE), F32),
            pltpu.VMEM((b, 2 * SLAB_STATE), F32),
        ],
        compiler_params=_params(("parallel", "arbitrary")),
        name="ssm_scan",
    )(u4, w_in, cs, a_t, dsk)
    return y4.reshape(lc, b, slabs, t, LANES)


def _glu_out_kernel(y_ref, z_ref, wg_ref, bg_ref, wo_ref, x_ref, gate_ref, g_ref, o_ref):
    slabs = y_ref.shape[1]
    tm = z_ref.shape[0]
    y = jnp.concatenate([y_ref[:, j].reshape(tm, LANES) for j in range(slabs)], axis=-1)
    g = jax.nn.gelu(y)
    t = jnp.dot(g.astype(BF16), wg_ref[...], preferred_element_type=F32) + bg_ref[...]
    y2 = g * jax.nn.sigmoid(t) * jax.nn.silu(z_ref[...].astype(F32))
    o = jnp.dot(y2.astype(BF16), wo_ref[...], preferred_element_type=F32)
    o_ref[...] = x_ref[...] + gate_ref[...] * _rms(o, g_ref[...])


def _glu_out(y5, z, w_glu, b_glu, w_out, x, gate, g_post, tm=512):
    b, l, d = x.shape
    _, _, slabs, t, _ = y5.shape
    width = slabs * LANES
    return pl.pallas_call(
        _glu_out_kernel,
        grid=(b, l // tm),
        in_specs=[
            pl.BlockSpec((tm // t, None, slabs, t, LANES), lambda bi, i: (i, bi, 0, 0, 0)),
            pl.BlockSpec((None, tm, width), lambda bi, i: (bi, i, 0)),
            pl.BlockSpec((width, width), lambda bi, i: (0, 0)),
            pl.BlockSpec((1, width), lambda bi, i: (0, 0)),
            pl.BlockSpec((width, d), lambda bi, i: (0, 0)),
            pl.BlockSpec((None, tm, d), lambda bi, i: (bi, i, 0)),
            pl.BlockSpec((None, 1, d), lambda bi, i: (bi, 0, 0)),
            pl.BlockSpec((1, d), lambda bi, i: (0, 0)),
        ],
        out_specs=pl.BlockSpec((None, tm, d), lambda bi, i: (bi, i, 0)),
        out_shape=jax.ShapeDtypeStruct((b, l, d), F32),
        compiler_params=_params(("parallel", "parallel")),
        name="glu_out",
    )(y5, z, w_glu, b_glu.reshape(1, width), w_out, x, gate.reshape(b, 1, d),
      g_post.reshape(1, d))


def _ab_layer(x, mod, g_pre, g_post, w_in, w_out, sgu_g, sgu_w, sgu_b):
    d = x.shape[-1]
    shift, scale, gate = mod[:, :d], mod[:, d:2 * d], mod[:, 2 * d:]
    wa = d // 2
    heads = wa // HEAD_DIM
    proj = _inproj(x, g_pre, scale, shift, w_in.astype(BF16))
    out_a = _sgu(proj, sgu_w, sgu_b, sgu_g, wa)
    attn_heads = math.gcd(heads, 8)
    blk = lambda cols: cols // (attn_heads * HEAD_DIM)
    out_b = _attention(proj, blk(3 * wa), blk(4 * wa), blk(5 * wa), blk(6 * wa), heads,
                       heads=attn_heads)
    return _ab_out(out_a, out_b, w_out.astype(BF16), x, gate, g_post)


def _ssm_layer(x, mod, g_pre, g_post, w_in, w_out, lam_re, lam_im, b_re, b_im, c_re, c_im,
               d_skip, log_dt, w_glu, b_glu):
    d = x.shape[-1]
    shift, scale, gate = mod[:, :d], mod[:, d:2 * d], mod[:, 2 * d:]
    u5, z = _inproj_ssm(x, g_pre, scale, shift, w_in.astype(BF16))
    y5 = _ssm(u5, *_ssm_prep(lam_re, lam_im, log_dt, b_re, b_im, c_re, c_im, d_skip))
    return _glu_out(y5, z, w_glu.astype(BF16), b_glu, w_out.astype(BF16), x, gate, g_post)


def kernel(x, c, ln_pre_g, ln_post_g, w_mod, b_mod, w_in_ab, w_out_ab, sgu_norm_g, sgu_w, sgu_b,
           w_in_ssm, w_out_ssm, lam_re, lam_im, b_re, b_im, c_re, c_im, d_skip, log_dt,
           w_glu, b_glu):
    depth = w_mod.shape[0]
    batch = x.shape[0]
    pad = (-batch) % 8
    c_pad = jnp.pad(c, ((0, pad), (0, 0)))
    mod = _adaln_mod(c_pad, w_mod, b_mod)[:, :batch]
    for layer in range(depth):
        i = layer // 2
        if layer % 2 == 0:
            x = _ab_layer(x, mod[layer], ln_pre_g[layer], ln_post_g[layer], w_in_ab[i],
                          w_out_ab[i], sgu_norm_g[i], sgu_w[i], sgu_b[i])
        else:
            x = _ssm_layer(x, mod[layer], ln_pre_g[layer], ln_post_g[layer], w_in_ssm[i],
                           w_out_ssm[i], lam_re[i], lam_im[i], b_re[i], b_im[i], c_re[i],
                           c_im[i], d_skip[i], log_dt[i], w_glu[i], b_glu[i])
    return x
```

```python
import functools
import math

import jax
import jax.numpy as jnp
from jax import lax
from jax.experimental import pallas as pl
from jax.experimental.pallas import tpu as pltpu

F32 = jnp.float32
BF16 = jnp.bfloat16

EPS = 1e-6
LANES = 128
HEAD_DIM = 128
SGU_CHUNK = 128
SSM_GROUP = 16
SSM_STATE = 64
SSM_T = 8
GROUPS_PER_SLAB = LANES // SSM_GROUP
SLAB_STATE = GROUPS_PER_SLAB * SSM_STATE
VMEM_LIMIT = 56 * 1024 * 1024
EXP_UNDERFLOW = -104.0


def _params(sem, vmem=VMEM_LIMIT):
    return pltpu.CompilerParams(dimension_semantics=sem, vmem_limit_bytes=vmem)


def _rms(y, g):
    ms = jnp.mean(y * y, axis=-1, keepdims=True)
    return y * lax.rsqrt(ms + EPS) * g


def _mod_kernel(c_ref, w_ref, b_ref, o_ref):
    cond = jax.nn.silu(c_ref[...])
    o_ref[...] = jnp.dot(cond.astype(BF16), w_ref[...].astype(BF16),
                         preferred_element_type=F32) + b_ref[...]


def _adaln_mod(c_pad, w_mod, b_mod, tn=512):
    depth, d, n = w_mod.shape
    rows = c_pad.shape[0]
    return pl.pallas_call(
        _mod_kernel,
        grid=(depth, n // tn),
        in_specs=[
            pl.BlockSpec((rows, d), lambda l, j: (0, 0)),
            pl.BlockSpec((None, d, tn), lambda l, j: (l, 0, j)),
            pl.BlockSpec((None, 1, tn), lambda l, j: (l, 0, j)),
        ],
        out_specs=pl.BlockSpec((None, rows, tn), lambda l, j: (l, 0, j)),
        out_shape=jax.ShapeDtypeStruct((depth, rows, n), F32),
        compiler_params=_params(("parallel", "parallel")),
        name="adaln_mod",
    )(c_pad, w_mod, b_mod.reshape(depth, 1, n))


def _prenorm(x_ref, g_ref, sc_ref, sh_ref):
    y = _rms(x_ref[...], g_ref[...])
    return (y * (1.0 + sc_ref[...]) + sh_ref[...]).astype(BF16)


def _inproj_kernel(x_ref, g_ref, sc_ref, sh_ref, w_ref, o_ref, h_scr):
    @pl.when(pl.program_id(2) == 0)
    def _():
        h_scr[...] = _prenorm(x_ref, g_ref, sc_ref, sh_ref)

    o_ref[...] = jnp.dot(h_scr[...], w_ref[...],
                         preferred_element_type=F32).astype(o_ref.dtype)


def _inproj(x, g, scale, shift, w, tm=1024, tn=1024):
    b, l, d = x.shape
    n = w.shape[1]
    tn = math.gcd(n, tn)
    return pl.pallas_call(
        _inproj_kernel,
        grid=(b, l // tm, n // tn),
        in_specs=[
            pl.BlockSpec((None, tm, d), lambda bi, i, j: (bi, i, 0)),
            pl.BlockSpec((1, d), lambda bi, i, j: (0, 0)),
            pl.BlockSpec((None, 1, d), lambda bi, i, j: (bi, 0, 0)),
            pl.BlockSpec((None, 1, d), lambda bi, i, j: (bi, 0, 0)),
            pl.BlockSpec((d, tn), lambda bi, i, j: (0, j)),
        ],
        out_specs=pl.BlockSpec((None, tm, tn), lambda bi, i, j: (bi, i, j)),
        out_shape=jax.ShapeDtypeStruct((b, l, n), BF16),
        scratch_shapes=[pltpu.VMEM((tm, d), BF16)],
        compiler_params=_params(("parallel", "parallel", "arbitrary")),
        name="inproj_ab",
    )(x, g.reshape(1, d), scale.reshape(b, 1, d), shift.reshape(b, 1, d), w)


def _inproj_ssm_kernel(x_ref, g_ref, sc_ref, sh_ref, w_ref, u_ref, z_ref, *, width):
    h = _prenorm(x_ref, g_ref, sc_ref, sh_ref)
    acc = jnp.dot(h, w_ref[...], preferred_element_type=F32)
    tm = acc.shape[0]
    for j in range(width // LANES):
        u_ref[:, j] = acc[:, j * LANES:(j + 1) * LANES].reshape(tm // SSM_T, SSM_T, LANES)
    z_ref[...] = acc[:, width:].astype(z_ref.dtype)


def _inproj_ssm(x, g, scale, shift, w, tm=512):
    b, l, d = x.shape
    width = w.shape[1] // 2
    slabs = width // LANES
    kern = functools.partial(_inproj_ssm_kernel, width=width)
    return pl.pallas_call(
        kern,
        grid=(b, l // tm),
        in_specs=[
            pl.BlockSpec((None, tm, d), lambda bi, i: (bi, i, 0)),
            pl.BlockSpec((1, d), lambda bi, i: (0, 0)),
            pl.BlockSpec((None, 1, d), lambda bi, i: (bi, 0, 0)),
            pl.BlockSpec((None, 1, d), lambda bi, i: (bi, 0, 0)),
            pl.BlockSpec((d, 2 * width), lambda bi, i: (0, 0)),
        ],
        out_specs=[
            pl.BlockSpec((tm // SSM_T, None, slabs, SSM_T, LANES),
                         lambda bi, i: (i, bi, 0, 0, 0)),
            pl.BlockSpec((None, tm, width), lambda bi, i: (bi, i, 0)),
        ],
        out_shape=[
            jax.ShapeDtypeStruct((l // SSM_T, b, slabs, SSM_T, LANES), F32),
            jax.ShapeDtypeStruct((b, l, width), BF16),
        ],
        compiler_params=_params(("parallel", "parallel")),
        name="inproj_ssm",
    )(x, g.reshape(1, d), scale.reshape(b, 1, d), shift.reshape(b, 1, d), w)


def _sgu_kernel(u_ref, v_ref, z_ref, w_ref, b_ref, g_ref, o_ref, *, heads, chunks):
    row = lax.broadcasted_iota(jnp.int32, (SGU_CHUNK, SGU_CHUNK), 0)
    col = lax.broadcasted_iota(jnp.int32, (SGU_CHUNK, SGU_CHUNK), 1)
    causal = col <= row
    for h in range(heads):
        w = jnp.where(causal, w_ref[h], 0.0).astype(BF16)
        bias = b_ref[h]
        gain = g_ref[:, h * HEAD_DIM:(h + 1) * HEAD_DIM]
        for c in range(chunks):
            rs = slice(c * SGU_CHUNK, (c + 1) * SGU_CHUNK)
            cs = slice(h * HEAD_DIM, (h + 1) * HEAD_DIM)
            v = jax.nn.gelu(v_ref[rs, cs].astype(F32))
            vn = _rms(v, gain)
            s = jnp.dot(w, vn.astype(BF16), preferred_element_type=F32) + bias
            u = jax.nn.gelu(u_ref[rs, cs].astype(F32))
            z = jax.nn.silu(z_ref[rs, cs].astype(F32))
            o_ref[rs, cs] = (u * s * z).astype(o_ref.dtype)


def _sgu(proj, sgu_w, sgu_b, sgu_g, width, tm=512):
    b, l, _ = proj.shape
    heads = width // HEAD_DIM

    def col_spec(k):
        return pl.BlockSpec((None, tm, width), lambda bi, i: (bi, i, k))

    kern = functools.partial(_sgu_kernel, heads=heads, chunks=tm // SGU_CHUNK)
    return pl.pallas_call(
        kern,
        grid=(b, l // tm),
        in_specs=[
            col_spec(0), col_spec(1), col_spec(2),
            pl.BlockSpec((heads, SGU_CHUNK, SGU_CHUNK), lambda bi, i: (0, 0, 0)),
            pl.BlockSpec((heads, SGU_CHUNK, 1), lambda bi, i: (0, 0, 0)),
            pl.BlockSpec((1, width), lambda bi, i: (0, 0)),
        ],
        out_specs=pl.BlockSpec((None, tm, width), lambda bi, i: (bi, i, 0)),
        out_shape=jax.ShapeDtypeStruct((b, l, width), BF16),
        compiler_params=_params(("parallel", "parallel")),
        name="sgu",
    )(proj, proj, proj, sgu_w, sgu_b.reshape(heads, SGU_CHUNK, 1), sgu_g.reshape(1, width))


def _attn_kernel(q_ref, k_ref, v_ref, bz_ref, o_ref, acc_scr, car_scr,
                 *, tq, tk, sub, heads, walk, scale):
    qi = pl.program_id(2)
    q0 = qi * tq
    r_i = lax.broadcasted_iota(jnp.int32, (sub, sub), 0)
    c_i = lax.broadcasted_iota(jnp.int32, (sub, sub), 1)
    later_mat = (r_i > c_i).astype(BF16)
    q_pos = q0 + lax.broadcasted_iota(jnp.int32, (tq, tk), 0)
    k_off = lax.broadcasted_iota(jnp.int32, (tq, tk), 1)

    acc_scr[...] = jnp.zeros_like(acc_scr)
    car_scr[...] = jnp.zeros_like(car_scr)

    def block(h, ks, width, limit):
        hc = slice(h * HEAD_DIM, (h + 1) * HEAD_DIM)
        ks = pl.multiple_of(ks, sub)
        q = q_ref[:, hc]
        k = k_ref[pl.ds(ks, width), hc]
        v = v_ref[pl.ds(ks, width), hc]
        z = lax.dot_general(q, k, (((1,), (1,)), ((), ())),
                            preferred_element_type=F32) * scale
        soft = jnp.log(1.0 + jnp.exp(-jnp.abs(z)))
        log_beta = jnp.minimum(z, 0.0) - soft
        log_keep = log_beta - z
        if limit is not None:
            mask = (ks + k_off) < limit
            log_keep = jnp.where(mask, log_keep, 0.0)
        carry = car_scr[h]
        n_sub = width // sub
        ws = [None] * n_sub
        for i in reversed(range(n_sub)):
            cols = slice(i * sub, (i + 1) * sub)
            lk = log_keep[:, cols]
            later = jnp.dot(lk.astype(BF16), later_mat, preferred_element_type=F32)
            ws[i] = jnp.exp(log_beta[:, cols] + later + carry)
            carry = carry + later[:, :1] + lk[:, :1]
        w = jnp.concatenate(ws, axis=1)
        if limit is not None:
            w = jnp.where(mask, w, 0.0)
        acc_scr[h] += jnp.dot(w.astype(BF16), v, preferred_element_type=F32)
        car_scr[h] = carry

    ks0 = jnp.maximum(q0 + tq - tk, 0)
    for h in range(heads):
        block(h, ks0, tk, q_pos)
    n_more = ks0 // sub

    for h0 in range(0, heads, walk):
        group = range(h0, min(h0 + walk, heads))

        def live(group=group):
            top = car_scr[group[0]]
            for h in group[1:]:
                top = jnp.maximum(top, car_scr[h])
            return jnp.max(top) > EXP_UNDERFLOW

        def cond(state):
            it, alive = state
            return jnp.logical_and(it < n_more, alive)

        def body(state, group=group, live=live):
            it, _ = state
            for h in group:
                block(h, ks0 - (it + 1) * sub, sub, None)
            return it + 1, live()

        lax.while_loop(cond, body, (jnp.int32(0), live()))

    gate = jax.nn.silu(bz_ref[...].astype(F32))
    acc = jnp.concatenate([acc_scr[h] for h in range(heads)], axis=-1)
    o_ref[...] = (acc * gate).astype(o_ref.dtype)


def _attention(proj, q_col, k_col, v_col, z_col, n_heads, tq=256, tk=512, sub=256, heads=2,
               walk=4):
    b, l, _ = proj.shape
    hw = heads * HEAD_DIM
    kern = functools.partial(_attn_kernel, tq=tq, tk=tk, sub=sub, heads=heads, walk=walk,
                             scale=1.0 / math.sqrt(HEAD_DIM))
    return pl.pallas_call(
        kern,
        grid=(b, n_heads // heads, l // tq),
        in_specs=[
            pl.BlockSpec((None, tq, hw), lambda bi, h, i: (bi, i, q_col + h)),
            pl.BlockSpec((None, l, hw), lambda bi, h, i: (bi, 0, k_col + h)),
            pl.BlockSpec((None, l, hw), lambda bi, h, i: (bi, 0, v_col + h)),
            pl.BlockSpec((None, tq, hw), lambda bi, h, i: (bi, i, z_col + h)),
        ],
        out_specs=pl.BlockSpec((None, tq, hw), lambda bi, h, i: (bi, i, h)),
        out_shape=jax.ShapeDtypeStruct((b, l, n_heads * HEAD_DIM), BF16),
        scratch_shapes=[pltpu.VMEM((heads, tq, HEAD_DIM), F32), pltpu.VMEM((heads, tq, 1), F32)],
        compiler_params=_params(("parallel", "parallel", "parallel")),
        name="stickbreak_attn",
    )(proj, proj, proj, proj)


def _ab_out_kernel(a_ref, b_ref, wa_ref, wb_ref, x_ref, gate_ref, g_ref, o_ref):
    y = (jnp.dot(a_ref[...], wa_ref[...], preferred_element_type=F32)
         + jnp.dot(b_ref[...], wb_ref[...], preferred_element_type=F32))
    o_ref[...] = x_ref[...] + gate_ref[...] * _rms(y, g_ref[...])


def _ab_out(out_a, out_b, w_out, x, gate, g_post, tm=512):
    b, l, d = x.shape
    wa = out_a.shape[-1]
    wb = out_b.shape[-1]
    assert wa == wb
    return pl.pallas_call(
        _ab_out_kernel,
        grid=(b, l // tm),
        in_specs=[
            pl.BlockSpec((None, tm, wa), lambda bi, i: (bi, i, 0)),
            pl.BlockSpec((None, tm, wb), lambda bi, i: (bi, i, 0)),
            pl.BlockSpec((wa, d), lambda bi, i: (0, 0)),
            pl.BlockSpec((wb, d), lambda bi, i: (1, 0)),
            pl.BlockSpec((None, tm, d), lambda bi, i: (bi, i, 0)),
            pl.BlockSpec((None, 1, d), lambda bi, i: (bi, 0, 0)),
            pl.BlockSpec((1, d), lambda bi, i: (0, 0)),
        ],
        out_specs=pl.BlockSpec((None, tm, d), lambda bi, i: (bi, i, 0)),
        out_shape=jax.ShapeDtypeStruct((b, l, d), F32),
        compiler_params=_params(("parallel", "parallel")),
        name="ab_out",
    )(out_a, out_b, w_out, w_out, x, gate.reshape(b, 1, d), g_post.reshape(1, d))


def _discretise(lr, li, ldt):
    dt = jnp.exp(ldt)
    mag = jnp.exp(lr * dt)
    return mag * jnp.cos(li * dt), mag * jnp.sin(li * dt)


def _cmul(x_re, x_im, y_re, y_im):
    return x_re * y_re - x_im * y_im, x_re * y_im + x_im * y_re


def _ssm_prep_kernel(lrc_ref, lic_ref, ldc_ref, br_ref, bi_ref, cr_ref, ci_ref,
                     lrp_ref, lip_ref, ldp_ref, ctr_ref, cti_ref,
                     lrl_ref, lil_ref, ldl_ref,
                     w_ref, cs_ref, at_ref):
    t_steps = SSM_T
    ns = SLAB_STATE
    out_w = t_steps * LANES
    nt = (((1,), (1,)), ((), ()))
    hp = lax.Precision.HIGHEST
    lr = lrc_ref[...]
    li = lic_ref[...]
    a_re, a_im = _discretise(lr, li, ldc_ref[...])
    den = lr * lr + li * li
    nr = a_re - 1.0
    coef_re = (nr * lr + a_im * li) / den
    coef_im = (a_im * lr - nr * li) / den
    bb_re, bb_im = _cmul(coef_re, coef_im, br_ref[...], bi_ref[...])
    row_g = lax.broadcasted_iota(jnp.int32, lr.shape, 0) // SSM_GROUP
    col_g = lax.broadcasted_iota(jnp.int32, lr.shape, 1) // SSM_STATE
    same_state = row_g == col_g
    row_k = lax.broadcasted_iota(jnp.int32, (LANES, LANES), 0) // SSM_GROUP
    col_k = lax.broadcasted_iota(jnp.int32, (LANES, LANES), 1) // SSM_GROUP
    same_lag = row_k == col_k
    c_re = cr_ref[...]
    c_im = ci_ref[...]
    a64_re = a_re[:, :SSM_STATE]
    a64_im = a_im[:, :SSM_STATE]
    bb64_re = bb_re[:, :SSM_STATE]
    bb64_im = bb_im[:, :SSM_STATE]
    p_re = jnp.ones_like(lr)
    p_im = jnp.zeros_like(lr)
    q_re = jnp.ones_like(a64_re)
    q_im = jnp.zeros_like(a64_re)
    lag_blocks = []
    for d in range(t_steps):
        s = t_steps - 1 - d
        bs_re, bs_im = _cmul(bb_re, bb_im, p_re, p_im)
        w_ref[s, :, out_w:out_w + ns] = jnp.where(same_state, bs_re, 0.0).astype(w_ref.dtype)
        w_ref[s, :, out_w + ns:] = jnp.where(same_state, bs_im, 0.0).astype(w_ref.dtype)
        cd_re, cd_im = _cmul(c_re, c_im, q_re, q_im)
        kd = (lax.dot_general(bb64_re, cd_re, nt, precision=hp, preferred_element_type=F32)
              - lax.dot_general(bb64_im, cd_im, nt, precision=hp, preferred_element_type=F32))
        lag_blocks.append(jnp.where(same_lag, kd, 0.0).astype(w_ref.dtype))
        p_re, p_im = _cmul(p_re, p_im, a_re, a_im)
        q_re, q_im = _cmul(q_re, q_im, a64_re, a64_im)
    zero_block = jnp.zeros((LANES, LANES), w_ref.dtype)
    for s in range(t_steps):
        for t in range(t_steps):
            w_ref[s, :, t * LANES:(t + 1) * LANES] = lag_blocks[t - s] if t >= s else zero_block
    ap_re, ap_im = _discretise(lrp_ref[...], lip_ref[...], ldp_ref[...])
    row_p = lax.broadcasted_iota(jnp.int32, ap_re.shape, 0) // SSM_STATE
    col_p = lax.broadcasted_iota(jnp.int32, ap_re.shape, 1) // SSM_GROUP
    same_out = row_p == col_p
    ct_re = ctr_ref[...]
    ct_im = cti_ref[...]
    r_re, r_im = ap_re, ap_im
    for t in range(t_steps):
        cd_re, cd_im = _cmul(ct_re, ct_im, r_re, r_im)
        cols = slice(t * LANES, (t + 1) * LANES)
        cs_ref[:ns, cols] = jnp.where(same_out, cd_re, 0.0).astype(cs_ref.dtype)
        cs_ref[ns:, cols] = jnp.where(same_out, -cd_im, 0.0).astype(cs_ref.dtype)
        r_re, r_im = _cmul(r_re, r_im, ap_re, ap_im)
    al_re, al_im = _discretise(lrl_ref[...], lil_ref[...], ldl_ref[...])
    t_re, t_im = al_re, al_im
    for _ in range(t_steps - 1):
        t_re, t_im = _cmul(t_re, t_im, al_re, al_im)
    at_ref[:, :ns] = t_re
    at_ref[:, ns:] = t_im


def _ssm_prep(lam_re, lam_im, log_dt, b_re, b_im, c_re, c_im, d_skip):
    g, p = lam_re.shape
    gs = GROUPS_PER_SLAB
    rows = g * SSM_GROUP
    slabs = rows // LANES
    ns = SLAB_STATE
    width = SSM_T * LANES
    ldt = jnp.broadcast_to(log_dt.reshape(g, 1), (g, p))
    chan = lambda a: jnp.tile(jnp.repeat(a, SSM_GROUP, axis=0), (1, gs))
    b_t = lambda a: jnp.tile(jnp.transpose(a, (0, 2, 1)).reshape(rows, p), (1, gs))
    stat = lambda a: jnp.broadcast_to(a.reshape(g * p, 1), (g * p, LANES))
    c_t = lambda a: jnp.tile(jnp.transpose(a, (0, 2, 1)).reshape(g * p, SSM_GROUP), (1, gs))
    lane = lambda a: a.reshape(slabs, 1, ns)
    chan_spec = pl.BlockSpec((LANES, ns), lambda j: (j, 0))
    stat_spec = pl.BlockSpec((ns, LANES), lambda j: (j, 0))
    lane_spec = pl.BlockSpec((None, 1, ns), lambda j: (j, 0, 0))
    w_in, cs, a_t = pl.pallas_call(
        _ssm_prep_kernel,
        grid=(slabs,),
        in_specs=[chan_spec] * 5 + [pl.BlockSpec((LANES, p), lambda j: (j, 0))] * 2
                 + [stat_spec] * 5 + [lane_spec] * 3,
        out_specs=[
            pl.BlockSpec((SSM_T, LANES, width + 2 * ns), lambda j: (0, j, 0)),
            pl.BlockSpec((None, 2 * ns, width), lambda j: (j, 0, 0)),
            pl.BlockSpec((None, 1, 2 * ns), lambda j: (j, 0, 0)),
        ],
        out_shape=[
            jax.ShapeDtypeStruct((SSM_T, rows, width + 2 * ns), BF16),
            jax.ShapeDtypeStruct((slabs, 2 * ns, width), BF16),
            jax.ShapeDtypeStruct((slabs, 1, 2 * ns), F32),
        ],
        compiler_params=_params(("parallel",)),
        name="ssm_prep",
    )(chan(lam_re), chan(lam_im), chan(ldt), b_t(b_re), b_t(b_im),
      c_re.reshape(rows, p), c_im.reshape(rows, p),
      stat(lam_re), stat(lam_im), stat(ldt), c_t(c_re), c_t(c_im),
      lane(lam_re), lane(lam_im), lane(ldt))
    dsk = jnp.tile(d_skip.reshape(slabs, 1, LANES), (1, 1, SSM_T))
    return w_in, cs, a_t, dsk


def _ssm_kernel(u_ref, w_ref, cs_ref, at_ref, dsk_ref, y_ref, s_scr, hp_scr, h_scr,
                *, batch, n_chunks):
    ns = SLAB_STATE
    out_w = SSM_T * LANES

    @pl.when(pl.program_id(1) == 0)
    def _():
        h_scr[...] = jnp.zeros_like(h_scr)

    u = jnp.concatenate([u_ref[:, s, :] for s in range(SSM_T)], axis=-1)
    w = w_ref[...].reshape(out_w, out_w + 2 * ns)
    r = jnp.dot(u.astype(BF16), w, preferred_element_type=F32)
    s_scr[...] = r[:, out_w:]
    a_re = jnp.broadcast_to(at_ref[:, :ns], (batch, ns))
    a_im = jnp.broadcast_to(at_ref[:, ns:], (batch, ns))

    sub = 8
    per_tile = sub // batch

    def step(k, carry):
        h_re, h_im = carry
        r0 = pl.multiple_of(k * sub, sub)
        s_re = s_scr[pl.ds(r0, sub), :ns]
        s_im = s_scr[pl.ds(r0, sub), ns:]
        prev_re, prev_im = [], []
        for c in range(per_tile):
            rows = slice(c * batch, (c + 1) * batch)
            prev_re.append(h_re)
            prev_im.append(h_im)
            h_re, h_im = (a_re * h_re - a_im * h_im + s_re[rows],
                          a_re * h_im + a_im * h_re + s_im[rows])
        hp_scr[pl.ds(r0, sub), :ns] = jnp.concatenate(prev_re, axis=0)
        hp_scr[pl.ds(r0, sub), ns:] = jnp.concatenate(prev_im, axis=0)
        return h_re, h_im

    h_re, h_im = lax.fori_loop(0, n_chunks // per_tile, step,
                               (h_scr[:, :ns], h_scr[:, ns:]))
    h_scr[:, :ns] = h_re
    h_scr[:, ns:] = h_im
    y = (r[:, :out_w]
         + jnp.dot(hp_scr[...].astype(BF16), cs_ref[...], preferred_element_type=F32)
         + dsk_ref[...] * u)
    for t in range(SSM_T):
        y_ref[:, t, :] = y[:, t * LANES:(t + 1) * LANES]


def _ssm(u5, w_in, cs, a_t, dsk, n_rows=512):
    lc, b, slabs, t, _ = u5.shape
    rows = lc * b
    width = t * LANES
    u4 = u5.reshape(rows, slabs, t, LANES)
    n_rows = min(n_rows, rows)
    kern = functools.partial(_ssm_kernel, batch=b, n_chunks=n_rows // b)
    y4 = pl.pallas_call(
        kern,
        grid=(slabs, rows // n_rows),
        in_specs=[
            pl.BlockSpec((n_rows, None, t, LANES), lambda j, i: (i, j, 0, 0)),
            pl.BlockSpec((t, LANES, width + 2 * SLAB_STATE), lambda j, i: (0, j, 0)),
            pl.BlockSpec((None, 2 * SLAB_STATE, width), lambda j, i: (j, 0, 0)),
            pl.BlockSpec((None, 1, 2 * SLAB_STATE), lambda j, i: (j, 0, 0)),
            pl.BlockSpec((None, 1, width), lambda j, i: (j, 0, 0)),
        ],
        out_specs=pl.BlockSpec((n_rows, None, t, LANES), lambda j, i: (i, j, 0, 0)),
        out_shape=jax.ShapeDtypeStruct((rows, slabs, t, LANES), F32),
        scratch_shapes=[
            pltpu.VMEM((n_rows, 2 * SLAB_STATE), F32),
            pltpu.VMEM((n_rows, 2 * SLAB_STATE), F32),
            pltpu.VMEM((b, 2 * SLAB_STATE), F32),
        ],
        compiler_params=_params(("parallel", "arbitrary")),
        name="ssm_scan",
    )(u4, w_in, cs, a_t, dsk)
    return y4.reshape(lc, b, slabs, t, LANES)


def _glu_out_kernel(y_ref, z_ref, wg_ref, bg_ref, wo_ref, x_ref, gate_ref, g_ref, o_ref):
    slabs = y_ref.shape[1]
    tm = z_ref.shape[0]
    y = jnp.concatenate([y_ref[:, j].reshape(tm, LANES) for j in range(slabs)], axis=-1)
    g = jax.nn.gelu(y)
    t = jnp.dot(g.astype(BF16), wg_ref[...], preferred_element_type=F32) + bg_ref[...]
    y2 = g * jax.nn.sigmoid(t) * jax.nn.silu(z_ref[...].astype(F32))
    o = jnp.dot(y2.astype(BF16), wo_ref[...], preferred_element_type=F32)
    o_ref[...] = x_ref[...] + gate_ref[...] * _rms(o, g_ref[...])


def _glu_out(y5, z, w_glu, b_glu, w_out, x, gate, g_post, tm=512):
    b, l, d = x.shape
    _, _, slabs, t, _ = y5.shape
    width = slabs * LANES
    return pl.pallas_call(
        _glu_out_kernel,
        grid=(b, l // tm),
        in_specs=[
            pl.BlockSpec((tm // t, None, slabs, t, LANES), lambda bi, i: (i, bi, 0, 0, 0)),
            pl.BlockSpec((None, tm, width), lambda bi, i: (bi, i, 0)),
            pl.BlockSpec((width, width), lambda bi, i: (0, 0)),
            pl.BlockSpec((1, width), lambda bi, i: (0, 0)),
            pl.BlockSpec((width, d), lambda bi, i: (0, 0)),
            pl.BlockSpec((None, tm, d), lambda bi, i: (bi, i, 0)),
            pl.BlockSpec((None, 1, d), lambda bi, i: (bi, 0, 0)),
            pl.BlockSpec((1, d), lambda bi, i: (0, 0)),
        ],
        out_specs=pl.BlockSpec((None, tm, d), lambda bi, i: (bi, i, 0)),
        out_shape=jax.ShapeDtypeStruct((b, l, d), F32),
        compiler_params=_params(("parallel", "parallel")),
        name="glu_out",
    )(y5, z, w_glu, b_glu.reshape(1, width), w_out, x, gate.reshape(b, 1, d),
      g_post.reshape(1, d))


def _ab_layer(x, mod, g_pre, g_post, w_in, w_out, sgu_g, sgu_w, sgu_b):
    d = x.shape[-1]
    shift, scale, gate = mod[:, :d], mod[:, d:2 * d], mod[:, 2 * d:]
    wa = d // 2
    heads = wa // HEAD_DIM
    proj = _inproj(x, g_pre, scale, shift, w_in.astype(BF16))
    out_a = _sgu(proj, sgu_w, sgu_b, sgu_g, wa)
    attn_heads = math.gcd(heads, 8)
    blk = lambda cols: cols // (attn_heads * HEAD_DIM)
    out_b = _attention(proj, blk(3 * wa), blk(4 * wa), blk(5 * wa), blk(6 * wa), heads,
                       heads=attn_heads)
    return _ab_out(out_a, out_b, w_out.astype(BF16), x, gate, g_post)


def _ssm_layer(x, mod, g_pre, g_post, w_in, w_out, lam_re, lam_im, b_re, b_im, c_re, c_im,
               d_skip, log_dt, w_glu, b_glu):
    d = x.shape[-1]
    shift, scale, gate = mod[:, :d], mod[:, d:2 * d], mod[:, 2 * d:]
    u5, z = _inproj_ssm(x, g_pre, scale, shift, w_in.astype(BF16))
    y5 = _ssm(u5, *_ssm_prep(lam_re, lam_im, log_dt, b_re, b_im, c_re, c_im, d_skip))
    return _glu_out(y5, z, w_glu.astype(BF16), b_glu, w_out.astype(BF16), x, gate, g_post)


def kernel(x, c, ln_pre_g, ln_post_g, w_mod, b_mod, w_in_ab, w_out_ab, sgu_norm_g, sgu_w, sgu_b,
           w_in_ssm, w_out_ssm, lam_re, lam_im, b_re, b_im, c_re, c_im, d_skip, log_dt,
           w_glu, b_glu):
    depth = w_mod.shape[0]
    batch = x.shape[0]
    pad = (-batch) % 8
    c_pad = jnp.pad(c, ((0, pad), (0, 0)))
    mod = _adaln_mod(c_pad, w_mod, b_mod)[:, :batch]
    for layer in range(depth):
        i = layer // 2
        if layer % 2 == 0:
            x = _ab_layer(x, mod[layer], ln_pre_g[layer], ln_post_g[layer], w_in_ab[i],
                          w_out_ab[i], sgu_norm_g[i], sgu_w[i], sgu_b[i])
        else:
            x = _ssm_layer(x, mod[layer], ln_pre_g[layer], ln_post_g[layer], w_in_ssm[i],
                           w_out_ssm[i], lam_re[i], lam_im[i], b_re[i], b_im[i], c_re[i],
                           c_im[i], d_skip[i], log_dt[i], w_glu[i], b_glu[i])
    return x
```

```python
import functools
import math

import jax
import jax.numpy as jnp
from jax import lax
from jax.experimental import pallas as pl
from jax.experimental.pallas import tpu as pltpu

F32 = jnp.float32
BF16 = jnp.bfloat16

EPS = 1e-6
LANES = 128
HEAD_DIM = 128
SGU_CHUNK = 128
SSM_GROUP = 16
SSM_STATE = 64
SSM_T = 8
GROUPS_PER_SLAB = LANES // SSM_GROUP
SLAB_STATE = GROUPS_PER_SLAB * SSM_STATE
VMEM_LIMIT = 56 * 1024 * 1024
EXP_UNDERFLOW = -104.0


def _params(sem, vmem=VMEM_LIMIT):
    return pltpu.CompilerParams(dimension_semantics=sem, vmem_limit_bytes=vmem)


def _rms(y, g):
    ms = jnp.mean(y * y, axis=-1, keepdims=True)
    return y * lax.rsqrt(ms + EPS) * g


def _mod_kernel(c_ref, w_ref, b_ref, o_ref):
    cond = jax.nn.silu(c_ref[...])
    o_ref[...] = jnp.dot(cond.astype(BF16), w_ref[...].astype(BF16),
                         preferred_element_type=F32) + b_ref[...]


def _adaln_mod(c_pad, w_mod, b_mod, tn=512):
    depth, d, n = w_mod.shape
    rows = c_pad.shape[0]
    return pl.pallas_call(
        _mod_kernel,
        grid=(depth, n // tn),
        in_specs=[
            pl.BlockSpec((rows, d), lambda l, j: (0, 0)),
            pl.BlockSpec((None, d, tn), lambda l, j: (l, 0, j)),
            pl.BlockSpec((None, 1, tn), lambda l, j: (l, 0, j)),
        ],
        out_specs=pl.BlockSpec((None, rows, tn), lambda l, j: (l, 0, j)),
        out_shape=jax.ShapeDtypeStruct((depth, rows, n), F32),
        compiler_params=_params(("parallel", "parallel")),
        name="adaln_mod",
    )(c_pad, w_mod, b_mod.reshape(depth, 1, n))


def _prenorm(x_ref, g_ref, sc_ref, sh_ref):
    y = _rms(x_ref[...], g_ref[...])
    return (y * (1.0 + sc_ref[...]) + sh_ref[...]).astype(BF16)


def _inproj_kernel(x_ref, g_ref, sc_ref, sh_ref, w_ref, o_ref, h_scr):
    @pl.when(pl.program_id(2) == 0)
    def _():
        h_scr[...] = _prenorm(x_ref, g_ref, sc_ref, sh_ref)

    o_ref[...] = jnp.dot(h_scr[...], w_ref[...],
                         preferred_element_type=F32).astype(o_ref.dtype)


def _inproj(x, g, scale, shift, w, tm=1024, tn=1024):
    b, l, d = x.shape
    n = w.shape[1]
    tn = math.gcd(n, tn)
    return pl.pallas_call(
        _inproj_kernel,
        grid=(b, l // tm, n // tn),
        in_specs=[
            pl.BlockSpec((None, tm, d), lambda bi, i, j: (bi, i, 0)),
            pl.BlockSpec((1, d), lambda bi, i, j: (0, 0)),
            pl.BlockSpec((None, 1, d), lambda bi, i, j: (bi, 0, 0)),
            pl.BlockSpec((None, 1, d), lambda bi, i, j: (bi, 0, 0)),
            pl.BlockSpec((d, tn), lambda bi, i, j: (0, j)),
        ],
        out_specs=pl.BlockSpec((None, tm, tn), lambda bi, i, j: (bi, i, j)),
        out_shape=jax.ShapeDtypeStruct((b, l, n), BF16),
        scratch_shapes=[pltpu.VMEM((tm, d), BF16)],
        compiler_params=_params(("parallel", "parallel", "arbitrary")),
        name="inproj_ab",
    )(x, g.reshape(1, d), scale.reshape(b, 1, d), shift.reshape(b, 1, d), w)


def _inproj_ssm_kernel(x_ref, g_ref, sc_ref, sh_ref, w_ref, u_ref, z_ref, *, width):
    h = _prenorm(x_ref, g_ref, sc_ref, sh_ref)
    acc = jnp.dot(h, w_ref[...], preferred_element_type=F32)
    tm = acc.shape[0]
    for j in range(width // LANES):
        u_ref[:, j] = acc[:, j * LANES:(j + 1) * LANES].reshape(tm // SSM_T, SSM_T, LANES)
    z_ref[...] = acc[:, width:].astype(z_ref.dtype)


def _inproj_ssm(x, g, scale, shift, w, tm=512):
    b, l, d = x.shape
    width = w.shape[1] // 2
    slabs = width // LANES
    kern = functools.partial(_inproj_ssm_kernel, width=width)
    return pl.pallas_call(
        kern,
        grid=(b, l // tm),
        in_specs=[
            pl.BlockSpec((None, tm, d), lambda bi, i: (bi, i, 0)),
            pl.BlockSpec((1, d), lambda bi, i: (0, 0)),
            pl.BlockSpec((None, 1, d), lambda bi, i: (bi, 0, 0)),
            pl.BlockSpec((None, 1, d), lambda bi, i: (bi, 0, 0)),
            pl.BlockSpec((d, 2 * width), lambda bi, i: (0, 0)),
        ],
        out_specs=[
            pl.BlockSpec((tm // SSM_T, None, slabs, SSM_T, LANES),
                         lambda bi, i: (i, bi, 0, 0, 0)),
            pl.BlockSpec((None, tm, width), lambda bi, i: (bi, i, 0)),
        ],
        out_shape=[
            jax.ShapeDtypeStruct((l // SSM_T, b, slabs, SSM_T, LANES), F32),
            jax.ShapeDtypeStruct((b, l, width), BF16),
        ],
        compiler_params=_params(("parallel", "parallel")),
        name="inproj_ssm",
    )(x, g.reshape(1, d), scale.reshape(b, 1, d), shift.reshape(b, 1, d), w)


def _sgu_kernel(u_ref, v_ref, z_ref, w_ref, b_ref, g_ref, o_ref, *, heads, chunks):
    row = lax.broadcasted_iota(jnp.int32, (SGU_CHUNK, SGU_CHUNK), 0)
    col = lax.broadcasted_iota(jnp.int32, (SGU_CHUNK, SGU_CHUNK), 1)
    causal = col <= row
    for h in range(heads):
        w = jnp.where(causal, w_ref[h], 0.0).astype(BF16)
        bias = b_ref[h]
        gain = g_ref[:, h * HEAD_DIM:(h + 1) * HEAD_DIM]
        for c in range(chunks):
            rs = slice(c * SGU_CHUNK, (c + 1) * SGU_CHUNK)
            cs = slice(h * HEAD_DIM, (h + 1) * HEAD_DIM)
            v = jax.nn.gelu(v_ref[rs, cs].astype(F32))
            vn = _rms(v, gain)
            s = jnp.dot(w, vn.astype(BF16), preferred_element_type=F32) + bias
            u = jax.nn.gelu(u_ref[rs, cs].astype(F32))
            z = jax.nn.silu(z_ref[rs, cs].astype(F32))
            o_ref[rs, cs] = (u * s * z).astype(o_ref.dtype)


def _sgu(proj, sgu_w, sgu_b, sgu_g, width, tm=512):
    b, l, _ = proj.shape
    heads = width // HEAD_DIM

    def col_spec(k):
        return pl.BlockSpec((None, tm, width), lambda bi, i: (bi, i, k))

    kern = functools.partial(_sgu_kernel, heads=heads, chunks=tm // SGU_CHUNK)
    return pl.pallas_call(
        kern,
        grid=(b, l // tm),
        in_specs=[
            col_spec(0), col_spec(1), col_spec(2),
            pl.BlockSpec((heads, SGU_CHUNK, SGU_CHUNK), lambda bi, i: (0, 0, 0)),
            pl.BlockSpec((heads, SGU_CHUNK, 1), lambda bi, i: (0, 0, 0)),
            pl.BlockSpec((1, width), lambda bi, i: (0, 0)),
        ],
        out_specs=pl.BlockSpec((None, tm, width), lambda bi, i: (bi, i, 0)),
        out_shape=jax.ShapeDtypeStruct((b, l, width), BF16),
        compiler_params=_params(("parallel", "parallel")),
        name="sgu",
    )(proj, proj, proj, sgu_w, sgu_b.reshape(heads, SGU_CHUNK, 1), sgu_g.reshape(1, width))


def _attn_kernel(q_ref, k_ref, v_ref, bz_ref, o_ref, acc_scr, car_scr,
                 *, tq, tk, sub, heads, walk, scale):
    qi = pl.program_id(2)
    q0 = qi * tq
    r_i = lax.broadcasted_iota(jnp.int32, (sub, sub), 0)
    c_i = lax.broadcasted_iota(jnp.int32, (sub, sub), 1)
    later_mat = (r_i > c_i).astype(BF16)
    q_pos = q0 + lax.broadcasted_iota(jnp.int32, (tq, tk), 0)
    k_off = lax.broadcasted_iota(jnp.int32, (tq, tk), 1)

    acc_scr[...] = jnp.zeros_like(acc_scr)
    car_scr[...] = jnp.zeros_like(car_scr)

    def block(h, ks, width, limit):
        hc = slice(h * HEAD_DIM, (h + 1) * HEAD_DIM)
        ks = pl.multiple_of(ks, sub)
        q = q_ref[:, hc]
        k = k_ref[pl.ds(ks, width), hc]
        v = v_ref[pl.ds(ks, width), hc]
        z = lax.dot_general(q, k, (((1,), (1,)), ((), ())),
                            preferred_element_type=F32) * scale
        soft = jnp.log(1.0 + jnp.exp(-jnp.abs(z)))
        log_beta = jnp.minimum(z, 0.0) - soft
        log_keep = log_beta - z
        if limit is not None:
            mask = (ks + k_off) < limit
            log_keep = jnp.where(mask, log_keep, 0.0)
        carry = car_scr[h]
        n_sub = width // sub
        ws = [None] * n_sub
        for i in reversed(range(n_sub)):
            cols = slice(i * sub, (i + 1) * sub)
            lk = log_keep[:, cols]
            later = jnp.dot(lk.astype(BF16), later_mat, preferred_element_type=F32)
            ws[i] = jnp.exp(log_beta[:, cols] + later + carry)
            carry = carry + later[:, :1] + lk[:, :1]
        w = jnp.concatenate(ws, axis=1)
        if limit is not None:
            w = jnp.where(mask, w, 0.0)
        acc_scr[h] += jnp.dot(w.astype(BF16), v, preferred_element_type=F32)
        car_scr[h] = carry

    ks0 = jnp.maximum(q0 + tq - tk, 0)
    for h in range(heads):
        block(h, ks0, tk, q_pos)
    n_more = ks0 // sub

    for h0 in range(0, heads, walk):
        group = range(h0, min(h0 + walk, heads))

        def live(group=group):
            top = car_scr[group[0]]
            for h in group[1:]:
                top = jnp.maximum(top, car_scr[h])
            return jnp.max(top) > EXP_UNDERFLOW

        def cond(state):
            it, alive = state
            return jnp.logical_and(it < n_more, alive)

        def body(state, group=group, live=live):
            it, _ = state
            for h in group:
                block(h, ks0 - (it + 1) * sub, sub, None)
            return it + 1, live()

        lax.while_loop(cond, body, (jnp.int32(0), live()))

    gate = jax.nn.silu(bz_ref[...].astype(F32))
    acc = jnp.concatenate([acc_scr[h] for h in range(heads)], axis=-1)
    o_ref[...] = (acc * gate).astype(o_ref.dtype)


def _attention(proj, q_col, k_col, v_col, z_col, n_heads, tq=256, tk=512, sub=256, heads=2,
               walk=4):
    b, l, _ = proj.shape
    hw = heads * HEAD_DIM
    kern = functools.partial(_attn_kernel, tq=tq, tk=tk, sub=sub, heads=heads, walk=walk,
                             scale=1.0 / math.sqrt(HEAD_DIM))
    return pl.pallas_call(
        kern,
        grid=(b, n_heads // heads, l // tq),
        in_specs=[
            pl.BlockSpec((None, tq, hw), lambda bi, h, i: (bi, i, q_col + h)),
            pl.BlockSpec((None, l, hw), lambda bi, h, i: (bi, 0, k_col + h)),
            pl.BlockSpec((None, l, hw), lambda bi, h, i: (bi, 0, v_col + h)),
            pl.BlockSpec((None, tq, hw), lambda bi, h, i: (bi, i, z_col + h)),
        ],
        out_specs=pl.BlockSpec((None, tq, hw), lambda bi, h, i: (bi, i, h)),
        out_shape=jax.ShapeDtypeStruct((b, l, n_heads * HEAD_DIM), BF16),
        scratch_shapes=[pltpu.VMEM((heads, tq, HEAD_DIM), F32), pltpu.VMEM((heads, tq, 1), F32)],
        compiler_params=_params(("parallel", "parallel", "parallel")),
        name="stickbreak_attn",
    )(proj, proj, proj, proj)


def _ab_out_kernel(a_ref, b_ref, wa_ref, wb_ref, x_ref, gate_ref, g_ref, o_ref):
    y = (jnp.dot(a_ref[...], wa_ref[...], preferred_element_type=F32)
         + jnp.dot(b_ref[...], wb_ref[...], preferred_element_type=F32))
    o_ref[...] = x_ref[...] + gate_ref[...] * _rms(y, g_ref[...])


def _ab_out(out_a, out_b, w_out, x, gate, g_post, tm=512):
    b, l, d = x.shape
    wa = out_a.shape[-1]
    wb = out_b.shape[-1]
    assert wa == wb
    return pl.pallas_call(
        _ab_out_kernel,
        grid=(b, l // tm),
        in_specs=[
            pl.BlockSpec((None, tm, wa), lambda bi, i: (bi, i, 0)),
            pl.BlockSpec((None, tm, wb), lambda bi, i: (bi, i, 0)),
            pl.BlockSpec((wa, d), lambda bi, i: (0, 0)),
            pl.BlockSpec((wb, d), lambda bi, i: (1, 0)),
            pl.BlockSpec((None, tm, d), lambda bi, i: (bi, i, 0)),
            pl.BlockSpec((None, 1, d), lambda bi, i: (bi, 0, 0)),
            pl.BlockSpec((1, d), lambda bi, i: (0, 0)),
        ],
        out_specs=pl.BlockSpec((None, tm, d), lambda bi, i: (bi, i, 0)),
        out_shape=jax.ShapeDtypeStruct((b, l, d), F32),
        compiler_params=_params(("parallel", "parallel")),
        name="ab_out",
    )(out_a, out_b, w_out, w_out, x, gate.reshape(b, 1, d), g_post.reshape(1, d))


def _discretise(lr, li, ldt):
    dt = jnp.exp(ldt)
    mag = jnp.exp(lr * dt)
    return mag * jnp.cos(li * dt), mag * jnp.sin(li * dt)


def _cmul(x_re, x_im, y_re, y_im):
    return x_re * y_re - x_im * y_im, x_re * y_im + x_im * y_re


def _ssm_prep_kernel(lrc_ref, lic_ref, ldc_ref, br_ref, bi_ref, cr_ref, ci_ref,
                     lrp_ref, lip_ref, ldp_ref, ctr_ref, cti_ref,
                     lrl_ref, lil_ref, ldl_ref,
                     w_ref, cs_ref, at_ref):
    t_steps = SSM_T
    ns = SLAB_STATE
    out_w = t_steps * LANES
    nt = (((1,), (1,)), ((), ()))
    hp = lax.Precision.HIGHEST
    lr = lrc_ref[...]
    li = lic_ref[...]
    a_re, a_im = _discretise(lr, li, ldc_ref[...])
    den = lr * lr + li * li
    nr = a_re - 1.0
    coef_re = (nr * lr + a_im * li) / den
    coef_im = (a_im * lr - nr * li) / den
    bb_re, bb_im = _cmul(coef_re, coef_im, br_ref[...], bi_ref[...])
    row_g = lax.broadcasted_iota(jnp.int32, lr.shape, 0) // SSM_GROUP
    col_g = lax.broadcasted_iota(jnp.int32, lr.shape, 1) // SSM_STATE
    same_state = row_g == col_g
    row_k = lax.broadcasted_iota(jnp.int32, (LANES, LANES), 0) // SSM_GROUP
    col_k = lax.broadcasted_iota(jnp.int32, (LANES, LANES), 1) // SSM_GROUP
    same_lag = row_k == col_k
    c_re = cr_ref[...]
    c_im = ci_ref[...]
    a64_re = a_re[:, :SSM_STATE]
    a64_im = a_im[:, :SSM_STATE]
    bb64_re = bb_re[:, :SSM_STATE]
    bb64_im = bb_im[:, :SSM_STATE]
    p_re = jnp.ones_like(lr)
    p_im = jnp.zeros_like(lr)
    q_re = jnp.ones_like(a64_re)
    q_im = jnp.zeros_like(a64_re)
    lag_blocks = []
    for d in range(t_steps):
        s = t_steps - 1 - d
        bs_re, bs_im = _cmul(bb_re, bb_im, p_re, p_im)
        w_ref[s, :, out_w:out_w + ns] = jnp.where(same_state, bs_re, 0.0).astype(w_ref.dtype)
        w_ref[s, :, out_w + ns:] = jnp.where(same_state, bs_im, 0.0).astype(w_ref.dtype)
        cd_re, cd_im = _cmul(c_re, c_im, q_re, q_im)
        kd = (lax.dot_general(bb64_re, cd_re, nt, precision=hp, preferred_element_type=F32)
              - lax.dot_general(bb64_im, cd_im, nt, precision=hp, preferred_element_type=F32))
        lag_blocks.append(jnp.where(same_lag, kd, 0.0).astype(w_ref.dtype))
        p_re, p_im = _cmul(p_re, p_im, a_re, a_im)
        q_re, q_im = _cmul(q_re, q_im, a64_re, a64_im)
    zero_block = jnp.zeros((LANES, LANES), w_ref.dtype)
    for s in range(t_steps):
        for t in range(t_steps):
            w_ref[s, :, t * LANES:(t + 1) * LANES] = lag_blocks[t - s] if t >= s else zero_block
    ap_re, ap_im = _discretise(lrp_ref[...], lip_ref[...], ldp_ref[...])
    row_p = lax.broadcasted_iota(jnp.int32, ap_re.shape, 0) // SSM_STATE
    col_p = lax.broadcasted_iota(jnp.int32, ap_re.shape, 1) // SSM_GROUP
    same_out = row_p == col_p
    ct_re = ctr_ref[...]
    ct_im = cti_ref[...]
    r_re, r_im = ap_re, ap_im
    for t in range(t_steps):
        cd_re, cd_im = _cmul(ct_re, ct_im, r_re, r_im)
        cols = slice(t * LANES, (t + 1) * LANES)
        cs_ref[:ns, cols] = jnp.where(same_out, cd_re, 0.0).astype(cs_ref.dtype)
        cs_ref[ns:, cols] = jnp.where(same_out, -cd_im, 0.0).astype(cs_ref.dtype)
        r_re, r_im = _cmul(r_re, r_im, ap_re, ap_im)
    al_re, al_im = _discretise(lrl_ref[...], lil_ref[...], ldl_ref[...])
    t_re, t_im = al_re, al_im
    for _ in range(t_steps - 1):
        t_re, t_im = _cmul(t_re, t_im, al_re, al_im)
    at_ref[:, :ns] = t_re
    at_ref[:, ns:] = t_im


def _ssm_prep(lam_re, lam_im, log_dt, b_re, b_im, c_re, c_im, d_skip):
    g, p = lam_re.shape
    gs = GROUPS_PER_SLAB
    rows = g * SSM_GROUP
    slabs = rows // LANES
    ns = SLAB_STATE
    width = SSM_T * LANES
    ldt = jnp.broadcast_to(log_dt.reshape(g, 1), (g, p))
    chan = lambda a: jnp.tile(jnp.repeat(a, SSM_GROUP, axis=0), (1, gs))
    b_t = lambda a: jnp.tile(jnp.transpose(a, (0, 2, 1)).reshape(rows, p), (1, gs))
    stat = lambda a: jnp.broadcast_to(a.reshape(g * p, 1), (g * p, LANES))
    c_t = lambda a: jnp.tile(jnp.transpose(a, (0, 2, 1)).reshape(g * p, SSM_GROUP), (1, gs))
    lane = lambda a: a.reshape(slabs, 1, ns)
    chan_spec = pl.BlockSpec((LANES, ns), lambda j: (j, 0))
    stat_spec = pl.BlockSpec((ns, LANES), lambda j: (j, 0))
    lane_spec = pl.BlockSpec((None, 1, ns), lambda j: (j, 0, 0))
    w_in, cs, a_t = pl.pallas_call(
        _ssm_prep_kernel,
        grid=(slabs,),
        in_specs=[chan_spec] * 5 + [pl.BlockSpec((LANES, p), lambda j: (j, 0))] * 2
                 + [stat_spec] * 5 + [lane_spec] * 3,
        out_specs=[
            pl.BlockSpec((SSM_T, LANES, width + 2 * ns), lambda j: (0, j, 0)),
            pl.BlockSpec((None, 2 * ns, width), lambda j: (j, 0, 0)),
            pl.BlockSpec((None, 1, 2 * ns), lambda j: (j, 0, 0)),
        ],
        out_shape=[
            jax.ShapeDtypeStruct((SSM_T, rows, width + 2 * ns), BF16),
            jax.ShapeDtypeStruct((slabs, 2 * ns, width), BF16),
            jax.ShapeDtypeStruct((slabs, 1, 2 * ns), F32),
        ],
        compiler_params=_params(("parallel",)),
        name="ssm_prep",
    )(chan(lam_re), chan(lam_im), chan(ldt), b_t(b_re), b_t(b_im),
      c_re.reshape(rows, p), c_im.reshape(rows, p),
      stat(lam_re), stat(lam_im), stat(ldt), c_t(c_re), c_t(c_im),
      lane(lam_re), lane(lam_im), lane(ldt))
    dsk = jnp.tile(d_skip.reshape(slabs, 1, LANES), (1, 1, SSM_T))
    return w_in, cs, a_t, dsk


def _ssm_kernel(u_ref, w_ref, cs_ref, at_ref, dsk_ref, y_ref, s_scr, hp_scr, h_scr,
                *, batch, n_chunks):
    ns = SLAB_STATE
    out_w = SSM_T * LANES

    @pl.when(pl.program_id(1) == 0)
    def _():
        h_scr[...] = jnp.zeros_like(h_scr)

    u = jnp.concatenate([u_ref[:, s, :] for s in range(SSM_T)], axis=-1)
    w = w_ref[...].reshape(out_w, out_w + 2 * ns)
    r = jnp.dot(u.astype(BF16), w, preferred_element_type=F32)
    s_scr[...] = r[:, out_w:]
    a_re = jnp.broadcast_to(at_ref[:, :ns], (batch, ns))
    a_im = jnp.broadcast_to(at_ref[:, ns:], (batch, ns))

    sub = 8
    per_tile = sub // batch

    def step(k, carry):
        h_re, h_im = carry
        r0 = pl.multiple_of(k * sub, sub)
        s_re = s_scr[pl.ds(r0, sub), :ns]
        s_im = s_scr[pl.ds(r0, sub), ns:]
        prev_re, prev_im = [], []
        for c in range(per_tile):
            rows = slice(c * batch, (c + 1) * batch)
            prev_re.append(h_re)
            prev_im.append(h_im)
            h_re, h_im = (a_re * h_re - a_im * h_im + s_re[rows],
                          a_re * h_im + a_im * h_re + s_im[rows])
        hp_scr[pl.ds(r0, sub), :ns] = jnp.concatenate(prev_re, axis=0)
        hp_scr[pl.ds(r0, sub), ns:] = jnp.concatenate(prev_im, axis=0)
        return h_re, h_im

    h_re, h_im = lax.fori_loop(0, n_chunks // per_tile, step,
                               (h_scr[:, :ns], h_scr[:, ns:]))
    h_scr[:, :ns] = h_re
    h_scr[:, ns:] = h_im
    y = (r[:, :out_w]
         + jnp.dot(hp_scr[...].astype(BF16), cs_ref[...], preferred_element_type=F32)
         + dsk_ref[...] * u)
    for t in range(SSM_T):
        y_ref[:, t, :] = y[:, t * LANES:(t + 1) * LANES]


def _ssm(u5, w_in, cs, a_t, dsk, n_rows=1024):
    lc, b, slabs, t, _ = u5.shape
    rows = lc * b
    width = t * LANES
    u4 = u5.reshape(rows, slabs, t, LANES)
    n_rows = min(n_rows, rows)
    kern = functools.partial(_ssm_kernel, batch=b, n_chunks=n_rows // b)
    y4 = pl.pallas_call(
        kern,
        grid=(slabs, rows // n_rows),
        in_specs=[
            pl.BlockSpec((n_rows, None, t, LANES), lambda j, i: (i, j, 0, 0)),
            pl.BlockSpec((t, LANES, width + 2 * SLAB_STATE), lambda j, i: (0, j, 0)),
            pl.BlockSpec((None, 2 * SLAB_STATE, width), lambda j, i: (j, 0, 0)),
            pl.BlockSpec((None, 1, 2 * SLAB_STATE), lambda j, i: (j, 0, 0)),
            pl.BlockSpec((None, 1, width), lambda j, i: (j, 0, 0)),
        ],
        out_specs=pl.BlockSpec((n_rows, None, t, LANES), lambda j, i: (i, j, 0, 0)),
        out_shape=jax.ShapeDtypeStruct((rows, slabs, t, LANES), F32),
        scratch_shapes=[
            pltpu.VMEM((n_rows, 2 * SLAB_STATE), F32),
            pltpu.VMEM((n_rows, 2 * SLAB_STATE), F32),
            pltpu.VMEM((b, 2 * SLAB_STATE), F32),
        ],
        compiler_params=_params(("parallel", "arbitrary")),
        name="ssm_scan",
    )(u4, w_in, cs, a_t, dsk)
    return y4.reshape(lc, b, slabs, t, LANES)


def _glu_out_kernel(y_ref, z_ref, wg_ref, bg_ref, wo_ref, x_ref, gate_ref, g_ref, o_ref):
    slabs = y_ref.shape[1]
    tm = z_ref.shape[0]
    half = tm // 2
    for r in range(2):
        rows = slice(r * half, (r + 1) * half)
        crow = slice(r * half // SSM_T, (r + 1) * half // SSM_T)
        y = jnp.concatenate([y_ref[crow, j].reshape(half, LANES) for j in range(slabs)], axis=-1)
        g = jax.nn.gelu(y)
        t = jnp.dot(g.astype(BF16), wg_ref[...], preferred_element_type=F32) + bg_ref[...]
        y2 = g * jax.nn.sigmoid(t) * jax.nn.silu(z_ref[rows, :].astype(F32))
        o = jnp.dot(y2.astype(BF16), wo_ref[...], preferred_element_type=F32)
        o_ref[rows, :] = x_ref[rows, :] + gate_ref[...] * _rms(o, g_ref[...])


def _glu_out(y5, z, w_glu, b_glu, w_out, x, gate, g_post, tm=512):
    b, l, d = x.shape
    _, _, slabs, t, _ = y5.shape
    width = slabs * LANES
    return pl.pallas_call(
        _glu_out_kernel,
        grid=(b, l // tm),
        in_specs=[
            pl.BlockSpec((tm // t, None, slabs, t, LANES), lambda bi, i: (i, bi, 0, 0, 0)),
            pl.BlockSpec((None, tm, width), lambda bi, i: (bi, i, 0)),
            pl.BlockSpec((width, width), lambda bi, i: (0, 0)),
            pl.BlockSpec((1, width), lambda bi, i: (0, 0)),
            pl.BlockSpec((width, d), lambda bi, i: (0, 0)),
            pl.BlockSpec((None, tm, d), lambda bi, i: (bi, i, 0)),
            pl.BlockSpec((None, 1, d), lambda bi, i: (bi, 0, 0)),
            pl.BlockSpec((1, d), lambda bi, i: (0, 0)),
        ],
        out_specs=pl.BlockSpec((None, tm, d), lambda bi, i: (bi, i, 0)),
        out_shape=jax.ShapeDtypeStruct((b, l, d), F32),
        compiler_params=_params(("parallel", "parallel")),
        name="glu_out",
    )(y5, z, w_glu, b_glu.reshape(1, width), w_out, x, gate.reshape(b, 1, d),
      g_post.reshape(1, d))


def _ab_layer(x, mod, g_pre, g_post, w_in, w_out, sgu_g, sgu_w, sgu_b):
    d = x.shape[-1]
    shift, scale, gate = mod[:, :d], mod[:, d:2 * d], mod[:, 2 * d:]
    wa = d // 2
    heads = wa // HEAD_DIM
    proj = _inproj(x, g_pre, scale, shift, w_in.astype(BF16))
    out_a = _sgu(proj, sgu_w, sgu_b, sgu_g, wa)
    attn_heads = math.gcd(heads, 8)
    blk = lambda cols: cols // (attn_heads * HEAD_DIM)
    out_b = _attention(proj, blk(3 * wa), blk(4 * wa), blk(5 * wa), blk(6 * wa), heads,
                       heads=attn_heads)
    return _ab_out(out_a, out_b, w_out.astype(BF16), x, gate, g_post)


def _ssm_layer(x, mod, g_pre, g_post, w_in, w_out, lam_re, lam_im, b_re, b_im, c_re, c_im,
               d_skip, log_dt, w_glu, b_glu):
    d = x.shape[-1]
    shift, scale, gate = mod[:, :d], mod[:, d:2 * d], mod[:, 2 * d:]
    u5, z = _inproj_ssm(x, g_pre, scale, shift, w_in.astype(BF16))
    y5 = _ssm(u5, *_ssm_prep(lam_re, lam_im, log_dt, b_re, b_im, c_re, c_im, d_skip))
    return _glu_out(y5, z, w_glu.astype(BF16), b_glu, w_out.astype(BF16), x, gate, g_post)


def kernel(x, c, ln_pre_g, ln_post_g, w_mod, b_mod, w_in_ab, w_out_ab, sgu_norm_g, sgu_w, sgu_b,
           w_in_ssm, w_out_ssm, lam_re, lam_im, b_re, b_im, c_re, c_im, d_skip, log_dt,
           w_glu, b_glu):
    depth = w_mod.shape[0]
    batch = x.shape[0]
    pad = (-batch) % 8
    c_pad = jnp.pad(c, ((0, pad), (0, 0)))
    mod = _adaln_mod(c_pad, w_mod, b_mod)[:, :batch]
    for layer in range(depth):
        i = layer // 2
        if layer % 2 == 0:
            x = _ab_layer(x, mod[layer], ln_pre_g[layer], ln_post_g[layer], w_in_ab[i],
                          w_out_ab[i], sgu_norm_g[i], sgu_w[i], sgu_b[i])
        else:
            x = _ssm_layer(x, mod[layer], ln_pre_g[layer], ln_post_g[layer], w_in_ssm[i],
                           w_out_ssm[i], lam_re[i], lam_im[i], b_re[i], b_im[i], c_re[i],
                           c_im[i], d_skip[i], log_dt[i], w_glu[i], b_glu[i])
    return x
```

```python
import functools
import math

import jax
import jax.numpy as jnp
from jax import lax
from jax.experimental import pallas as pl
from jax.experimental.pallas import tpu as pltpu

F32 = jnp.float32
BF16 = jnp.bfloat16

EPS = 1e-6
LANES = 128
HEAD_DIM = 128
SGU_CHUNK = 128
SSM_GROUP = 16
SSM_STATE = 64
SSM_T = 8
GROUPS_PER_SLAB = LANES // SSM_GROUP
SLAB_STATE = GROUPS_PER_SLAB * SSM_STATE
VMEM_LIMIT = 56 * 1024 * 1024
EXP_UNDERFLOW = -104.0


def _params(sem, vmem=VMEM_LIMIT):
    return pltpu.CompilerParams(dimension_semantics=sem, vmem_limit_bytes=vmem)


def _rms(y, g):
    ms = jnp.mean(y * y, axis=-1, keepdims=True)
    return y * lax.rsqrt(ms + EPS) * g


def _mod_kernel(c_ref, w_ref, b_ref, o_ref):
    cond = jax.nn.silu(c_ref[...])
    o_ref[...] = jnp.dot(cond.astype(BF16), w_ref[...].astype(BF16),
                         preferred_element_type=F32) + b_ref[...]


def _adaln_mod(c_pad, w_mod, b_mod, tn=512):
    depth, d, n = w_mod.shape
    rows = c_pad.shape[0]
    return pl.pallas_call(
        _mod_kernel,
        grid=(depth, n // tn),
        in_specs=[
            pl.BlockSpec((rows, d), lambda l, j: (0, 0)),
            pl.BlockSpec((None, d, tn), lambda l, j: (l, 0, j)),
            pl.BlockSpec((None, 1, tn), lambda l, j: (l, 0, j)),
        ],
        out_specs=pl.BlockSpec((None, rows, tn), lambda l, j: (l, 0, j)),
        out_shape=jax.ShapeDtypeStruct((depth, rows, n), F32),
        compiler_params=_params(("parallel", "parallel")),
        name="adaln_mod",
    )(c_pad, w_mod, b_mod.reshape(depth, 1, n))


def _prenorm(x_ref, g_ref, sc_ref, sh_ref):
    y = _rms(x_ref[...], g_ref[...])
    return (y * (1.0 + sc_ref[...]) + sh_ref[...]).astype(BF16)


def _inproj_kernel(x_ref, g_ref, sc_ref, sh_ref, w_ref, o_ref, h_scr):
    @pl.when(pl.program_id(2) == 0)
    def _():
        h_scr[...] = _prenorm(x_ref, g_ref, sc_ref, sh_ref)

    o_ref[...] = jnp.dot(h_scr[...], w_ref[...],
                         preferred_element_type=F32).astype(o_ref.dtype)


def _inproj(x, g, scale, shift, w, tm=1024, tn=1024):
    b, l, d = x.shape
    n = w.shape[1]
    tn = math.gcd(n, tn)
    return pl.pallas_call(
        _inproj_kernel,
        grid=(b, l // tm, n // tn),
        in_specs=[
            pl.BlockSpec((None, tm, d), lambda bi, i, j: (bi, i, 0)),
            pl.BlockSpec((1, d), lambda bi, i, j: (0, 0)),
            pl.BlockSpec((None, 1, d), lambda bi, i, j: (bi, 0, 0)),
            pl.BlockSpec((None, 1, d), lambda bi, i, j: (bi, 0, 0)),
            pl.BlockSpec((d, tn), lambda bi, i, j: (0, j)),
        ],
        out_specs=pl.BlockSpec((None, tm, tn), lambda bi, i, j: (bi, i, j)),
        out_shape=jax.ShapeDtypeStruct((b, l, n), BF16),
        scratch_shapes=[pltpu.VMEM((tm, d), BF16)],
        compiler_params=_params(("parallel", "parallel", "arbitrary")),
        name="inproj_ab",
    )(x, g.reshape(1, d), scale.reshape(b, 1, d), shift.reshape(b, 1, d), w)


def _inproj_ssm_kernel(x_ref, g_ref, sc_ref, sh_ref, w_ref, u_ref, z_ref, *, width):
    h = _prenorm(x_ref, g_ref, sc_ref, sh_ref)
    acc = jnp.dot(h, w_ref[...], preferred_element_type=F32)
    tm = acc.shape[0]
    for j in range(width // LANES):
        u_ref[:, j] = acc[:, j * LANES:(j + 1) * LANES].reshape(tm // SSM_T, SSM_T, LANES)
    z_ref[...] = acc[:, width:].astype(z_ref.dtype)


def _inproj_ssm(x, g, scale, shift, w, tm=512):
    b, l, d = x.shape
    width = w.shape[1] // 2
    slabs = width // LANES
    kern = functools.partial(_inproj_ssm_kernel, width=width)
    return pl.pallas_call(
        kern,
        grid=(b, l // tm),
        in_specs=[
            pl.BlockSpec((None, tm, d), lambda bi, i: (bi, i, 0)),
            pl.BlockSpec((1, d), lambda bi, i: (0, 0)),
            pl.BlockSpec((None, 1, d), lambda bi, i: (bi, 0, 0)),
            pl.BlockSpec((None, 1, d), lambda bi, i: (bi, 0, 0)),
            pl.BlockSpec((d, 2 * width), lambda bi, i: (0, 0)),
        ],
        out_specs=[
            pl.BlockSpec((tm // SSM_T, None, slabs, SSM_T, LANES),
                         lambda bi, i: (i, bi, 0, 0, 0)),
            pl.BlockSpec((None, tm, width), lambda bi, i: (bi, i, 0)),
        ],
        out_shape=[
            jax.ShapeDtypeStruct((l // SSM_T, b, slabs, SSM_T, LANES), F32),
            jax.ShapeDtypeStruct((b, l, width), BF16),
        ],
        compiler_params=_params(("parallel", "parallel")),
        name="inproj_ssm",
    )(x, g.reshape(1, d), scale.reshape(b, 1, d), shift.reshape(b, 1, d), w)


def _sgu_kernel(u_ref, v_ref, z_ref, w_ref, b_ref, g_ref, o_ref, *, heads, chunks):
    row = lax.broadcasted_iota(jnp.int32, (SGU_CHUNK, SGU_CHUNK), 0)
    col = lax.broadcasted_iota(jnp.int32, (SGU_CHUNK, SGU_CHUNK), 1)
    causal = col <= row
    for h in range(heads):
        w = jnp.where(causal, w_ref[h], 0.0).astype(BF16)
        bias = b_ref[h]
        gain = g_ref[:, h * HEAD_DIM:(h + 1) * HEAD_DIM]
        for c in range(chunks):
            rs = slice(c * SGU_CHUNK, (c + 1) * SGU_CHUNK)
            cs = slice(h * HEAD_DIM, (h + 1) * HEAD_DIM)
            v = jax.nn.gelu(v_ref[rs, cs].astype(F32))
            vn = _rms(v, gain)
            s = jnp.dot(w, vn.astype(BF16), preferred_element_type=F32) + bias
            u = jax.nn.gelu(u_ref[rs, cs].astype(F32))
            z = jax.nn.silu(z_ref[rs, cs].astype(F32))
            o_ref[rs, cs] = (u * s * z).astype(o_ref.dtype)


def _sgu(proj, sgu_w, sgu_b, sgu_g, width, tm=512):
    b, l, _ = proj.shape
    heads = width // HEAD_DIM

    def col_spec(k):
        return pl.BlockSpec((None, tm, width), lambda bi, i: (bi, i, k))

    kern = functools.partial(_sgu_kernel, heads=heads, chunks=tm // SGU_CHUNK)
    return pl.pallas_call(
        kern,
        grid=(b, l // tm),
        in_specs=[
            col_spec(0), col_spec(1), col_spec(2),
            pl.BlockSpec((heads, SGU_CHUNK, SGU_CHUNK), lambda bi, i: (0, 0, 0)),
            pl.BlockSpec((heads, SGU_CHUNK, 1), lambda bi, i: (0, 0, 0)),
            pl.BlockSpec((1, width), lambda bi, i: (0, 0)),
        ],
        out_specs=pl.BlockSpec((None, tm, width), lambda bi, i: (bi, i, 0)),
        out_shape=jax.ShapeDtypeStruct((b, l, width), BF16),
        compiler_params=_params(("parallel", "parallel")),
        name="sgu",
    )(proj, proj, proj, sgu_w, sgu_b.reshape(heads, SGU_CHUNK, 1), sgu_g.reshape(1, width))


def _attn_kernel(q_ref, k_ref, v_ref, bz_ref, o_ref, acc_scr, car_scr,
                 *, tq, tk, sub, heads, walk, scale):
    qi = pl.program_id(2)
    q0 = qi * tq
    r_i = lax.broadcasted_iota(jnp.int32, (sub, sub), 0)
    c_i = lax.broadcasted_iota(jnp.int32, (sub, sub), 1)
    later_mat = (r_i > c_i).astype(BF16)
    q_pos = q0 + lax.broadcasted_iota(jnp.int32, (tq, tk), 0)
    k_off = lax.broadcasted_iota(jnp.int32, (tq, tk), 1)

    acc_scr[...] = jnp.zeros_like(acc_scr)
    car_scr[...] = jnp.zeros_like(car_scr)

    def block(h, ks, width, limit):
        hc = slice(h * HEAD_DIM, (h + 1) * HEAD_DIM)
        ks = pl.multiple_of(ks, sub)
        q = q_ref[:, hc]
        k = k_ref[pl.ds(ks, width), hc]
        v = v_ref[pl.ds(ks, width), hc]
        z = lax.dot_general(q, k, (((1,), (1,)), ((), ())),
                            preferred_element_type=F32) * scale
        soft = jnp.log(1.0 + jnp.exp(-jnp.abs(z)))
        log_beta = jnp.minimum(z, 0.0) - soft
        log_keep = log_beta - z
        if limit is not None:
            mask = (ks + k_off) < limit
            log_keep = jnp.where(mask, log_keep, 0.0)
        carry = car_scr[h]
        n_sub = width // sub
        ws = [None] * n_sub
        for i in reversed(range(n_sub)):
            cols = slice(i * sub, (i + 1) * sub)
            lk = log_keep[:, cols]
            later = jnp.dot(lk.astype(BF16), later_mat, preferred_element_type=F32)
            ws[i] = jnp.exp(log_beta[:, cols] + later + carry)
            carry = carry + later[:, :1] + lk[:, :1]
        w = jnp.concatenate(ws, axis=1)
        if limit is not None:
            w = jnp.where(mask, w, 0.0)
        acc_scr[h] += jnp.dot(w.astype(BF16), v, preferred_element_type=F32)
        car_scr[h] = carry

    ks0 = jnp.maximum(q0 + tq - tk, 0)
    for h in range(heads):
        block(h, ks0, tk, q_pos)
    n_more = ks0 // sub

    for h0 in range(0, heads, walk):
        group = range(h0, min(h0 + walk, heads))

        def live(group=group):
            top = car_scr[group[0]]
            for h in group[1:]:
                top = jnp.maximum(top, car_scr[h])
            return jnp.max(top) > EXP_UNDERFLOW

        def cond(state):
            it, alive = state
            return jnp.logical_and(it < n_more, alive)

        def body(state, group=group, live=live):
            it, _ = state
            for h in group:
                block(h, ks0 - (it + 1) * sub, sub, None)
            return it + 1, live()

        lax.while_loop(cond, body, (jnp.int32(0), live()))

    gate = jax.nn.silu(bz_ref[...].astype(F32))
    acc = jnp.concatenate([acc_scr[h] for h in range(heads)], axis=-1)
    o_ref[...] = (acc * gate).astype(o_ref.dtype)


def _attention(proj, q_col, k_col, v_col, z_col, n_heads, tq=256, tk=512, sub=256, heads=2,
               walk=8):
    b, l, _ = proj.shape
    hw = heads * HEAD_DIM
    kern = functools.partial(_attn_kernel, tq=tq, tk=tk, sub=sub, heads=heads, walk=walk,
                             scale=1.0 / math.sqrt(HEAD_DIM))
    return pl.pallas_call(
        kern,
        grid=(b, n_heads // heads, l // tq),
        in_specs=[
            pl.BlockSpec((None, tq, hw), lambda bi, h, i: (bi, i, q_col + h)),
            pl.BlockSpec((None, l, hw), lambda bi, h, i: (bi, 0, k_col + h)),
            pl.BlockSpec((None, l, hw), lambda bi, h, i: (bi, 0, v_col + h)),
            pl.BlockSpec((None, tq, hw), lambda bi, h, i: (bi, i, z_col + h)),
        ],
        out_specs=pl.BlockSpec((None, tq, hw), lambda bi, h, i: (bi, i, h)),
        out_shape=jax.ShapeDtypeStruct((b, l, n_heads * HEAD_DIM), BF16),
        scratch_shapes=[pltpu.VMEM((heads, tq, HEAD_DIM), F32), pltpu.VMEM((heads, tq, 1), F32)],
        compiler_params=_params(("parallel", "parallel", "parallel")),
        name="stickbreak_attn",
    )(proj, proj, proj, proj)


def _ab_out_kernel(a_ref, b_ref, wa_ref, wb_ref, x_ref, gate_ref, g_ref, o_ref):
    y = (jnp.dot(a_ref[...], wa_ref[...], preferred_element_type=F32)
         + jnp.dot(b_ref[...], wb_ref[...], preferred_element_type=F32))
    o_ref[...] = x_ref[...] + gate_ref[...] * _rms(y, g_ref[...])


def _ab_out(out_a, out_b, w_out, x, gate, g_post, tm=512):
    b, l, d = x.shape
    wa = out_a.shape[-1]
    wb = out_b.shape[-1]
    assert wa == wb
    return pl.pallas_call(
        _ab_out_kernel,
        grid=(b, l // tm),
        in_specs=[
            pl.BlockSpec((None, tm, wa), lambda bi, i: (bi, i, 0)),
            pl.BlockSpec((None, tm, wb), lambda bi, i: (bi, i, 0)),
            pl.BlockSpec((wa, d), lambda bi, i: (0, 0)),
            pl.BlockSpec((wb, d), lambda bi, i: (1, 0)),
            pl.BlockSpec((None, tm, d), lambda bi, i: (bi, i, 0)),
            pl.BlockSpec((None, 1, d), lambda bi, i: (bi, 0, 0)),
            pl.BlockSpec((1, d), lambda bi, i: (0, 0)),
        ],
        out_specs=pl.BlockSpec((None, tm, d), lambda bi, i: (bi, i, 0)),
        out_shape=jax.ShapeDtypeStruct((b, l, d), F32),
        compiler_params=_params(("parallel", "parallel")),
        name="ab_out",
    )(out_a, out_b, w_out, w_out, x, gate.reshape(b, 1, d), g_post.reshape(1, d))


def _discretise(lr, li, ldt):
    dt = jnp.exp(ldt)
    mag = jnp.exp(lr * dt)
    return mag * jnp.cos(li * dt), mag * jnp.sin(li * dt)


def _cmul(x_re, x_im, y_re, y_im):
    return x_re * y_re - x_im * y_im, x_re * y_im + x_im * y_re


def _ssm_prep_kernel(lrc_ref, lic_ref, ldc_ref, br_ref, bi_ref, cr_ref, ci_ref,
                     lrp_ref, lip_ref, ldp_ref, ctr_ref, cti_ref,
                     lrl_ref, lil_ref, ldl_ref,
                     w_ref, cs_ref, at_ref):
    t_steps = SSM_T
    ns = SLAB_STATE
    out_w = t_steps * LANES
    nt = (((1,), (1,)), ((), ()))
    hp = lax.Precision.HIGHEST
    lr = lrc_ref[...]
    li = lic_ref[...]
    a_re, a_im = _discretise(lr, li, ldc_ref[...])
    den = lr * lr + li * li
    nr = a_re - 1.0
    coef_re = (nr * lr + a_im * li) / den
    coef_im = (a_im * lr - nr * li) / den
    bb_re, bb_im = _cmul(coef_re, coef_im, br_ref[...], bi_ref[...])
    row_g = lax.broadcasted_iota(jnp.int32, lr.shape, 0) // SSM_GROUP
    col_g = lax.broadcasted_iota(jnp.int32, lr.shape, 1) // SSM_STATE
    same_state = row_g == col_g
    row_k = lax.broadcasted_iota(jnp.int32, (LANES, LANES), 0) // SSM_GROUP
    col_k = lax.broadcasted_iota(jnp.int32, (LANES, LANES), 1) // SSM_GROUP
    same_lag = row_k == col_k
    c_re = cr_ref[...]
    c_im = ci_ref[...]
    a64_re = a_re[:, :SSM_STATE]
    a64_im = a_im[:, :SSM_STATE]
    bb64_re = bb_re[:, :SSM_STATE]
    bb64_im = bb_im[:, :SSM_STATE]
    p_re = jnp.ones_like(lr)
    p_im = jnp.zeros_like(lr)
    q_re = jnp.ones_like(a64_re)
    q_im = jnp.zeros_like(a64_re)
    lag_blocks = []
    for d in range(t_steps):
        s = t_steps - 1 - d
        bs_re, bs_im = _cmul(bb_re, bb_im, p_re, p_im)
        w_ref[s, :, out_w:out_w + ns] = jnp.where(same_state, bs_re, 0.0).astype(w_ref.dtype)
        w_ref[s, :, out_w + ns:] = jnp.where(same_state, bs_im, 0.0).astype(w_ref.dtype)
        cd_re, cd_im = _cmul(c_re, c_im, q_re, q_im)
        kd = (lax.dot_general(bb64_re, cd_re, nt, precision=hp, preferred_element_type=F32)
              - lax.dot_general(bb64_im, cd_im, nt, precision=hp, preferred_element_type=F32))
        lag_blocks.append(jnp.where(same_lag, kd, 0.0).astype(w_ref.dtype))
        p_re, p_im = _cmul(p_re, p_im, a_re, a_im)
        q_re, q_im = _cmul(q_re, q_im, a64_re, a64_im)
    zero_block = jnp.zeros((LANES, LANES), w_ref.dtype)
    for s in range(t_steps):
        for t in range(t_steps):
            w_ref[s, :, t * LANES:(t + 1) * LANES] = lag_blocks[t - s] if t >= s else zero_block
    ap_re, ap_im = _discretise(lrp_ref[...], lip_ref[...], ldp_ref[...])
    row_p = lax.broadcasted_iota(jnp.int32, ap_re.shape, 0) // SSM_STATE
    col_p = lax.broadcasted_iota(jnp.int32, ap_re.shape, 1) // SSM_GROUP
    same_out = row_p == col_p
    ct_re = ctr_ref[...]
    ct_im = cti_ref[...]
    r_re, r_im = ap_re, ap_im
    for t in range(t_steps):
        cd_re, cd_im = _cmul(ct_re, ct_im, r_re, r_im)
        cols = slice(t * LANES, (t + 1) * LANES)
        cs_ref[:ns, cols] = jnp.where(same_out, cd_re, 0.0).astype(cs_ref.dtype)
        cs_ref[ns:, cols] = jnp.where(same_out, -cd_im, 0.0).astype(cs_ref.dtype)
        r_re, r_im = _cmul(r_re, r_im, ap_re, ap_im)
    al_re, al_im = _discretise(lrl_ref[...], lil_ref[...], ldl_ref[...])
    t_re, t_im = al_re, al_im
    for _ in range(t_steps - 1):
        t_re, t_im = _cmul(t_re, t_im, al_re, al_im)
    at_ref[:, :ns] = t_re
    at_ref[:, ns:] = t_im


def _ssm_prep(lam_re, lam_im, log_dt, b_re, b_im, c_re, c_im, d_skip):
    g, p = lam_re.shape
    gs = GROUPS_PER_SLAB
    rows = g * SSM_GROUP
    slabs = rows // LANES
    ns = SLAB_STATE
    width = SSM_T * LANES
    ldt = jnp.broadcast_to(log_dt.reshape(g, 1), (g, p))
    chan = lambda a: jnp.tile(jnp.repeat(a, SSM_GROUP, axis=0), (1, gs))
    b_t = lambda a: jnp.tile(jnp.transpose(a, (0, 2, 1)).reshape(rows, p), (1, gs))
    stat = lambda a: jnp.broadcast_to(a.reshape(g * p, 1), (g * p, LANES))
    c_t = lambda a: jnp.tile(jnp.transpose(a, (0, 2, 1)).reshape(g * p, SSM_GROUP), (1, gs))
    lane = lambda a: a.reshape(slabs, 1, ns)
    chan_spec = pl.BlockSpec((LANES, ns), lambda j: (j, 0))
    stat_spec = pl.BlockSpec((ns, LANES), lambda j: (j, 0))
    lane_spec = pl.BlockSpec((None, 1, ns), lambda j: (j, 0, 0))
    w_in, cs, a_t = pl.pallas_call(
        _ssm_prep_kernel,
        grid=(slabs,),
        in_specs=[chan_spec] * 5 + [pl.BlockSpec((LANES, p), lambda j: (j, 0))] * 2
                 + [stat_spec] * 5 + [lane_spec] * 3,
        out_specs=[
            pl.BlockSpec((SSM_T, LANES, width + 2 * ns), lambda j: (0, j, 0)),
            pl.BlockSpec((None, 2 * ns, width), lambda j: (j, 0, 0)),
            pl.BlockSpec((None, 1, 2 * ns), lambda j: (j, 0, 0)),
        ],
        out_shape=[
            jax.ShapeDtypeStruct((SSM_T, rows, width + 2 * ns), BF16),
            jax.ShapeDtypeStruct((slabs, 2 * ns, width), BF16),
            jax.ShapeDtypeStruct((slabs, 1, 2 * ns), F32),
        ],
        compiler_params=_params(("parallel",)),
        name="ssm_prep",
    )(chan(lam_re), chan(lam_im), chan(ldt), b_t(b_re), b_t(b_im),
      c_re.reshape(rows, p), c_im.reshape(rows, p),
      stat(lam_re), stat(lam_im), stat(ldt), c_t(c_re), c_t(c_im),
      lane(lam_re), lane(lam_im), lane(ldt))
    dsk = jnp.tile(d_skip.reshape(slabs, 1, LANES), (1, 1, SSM_T))
    return w_in, cs, a_t, dsk


def _ssm_kernel(u_ref, w_ref, cs_ref, at_ref, dsk_ref, y_ref, s_scr, hp_scr, h_scr,
                *, batch, n_chunks):
    ns = SLAB_STATE
    out_w = SSM_T * LANES

    @pl.when(pl.program_id(1) == 0)
    def _():
        h_scr[...] = jnp.zeros_like(h_scr)

    u = jnp.concatenate([u_ref[:, s, :] for s in range(SSM_T)], axis=-1)
    w = w_ref[...].reshape(out_w, out_w + 2 * ns)
    r = jnp.dot(u.astype(BF16), w, preferred_element_type=F32)
    s_scr[...] = r[:, out_w:]
    a_re = jnp.broadcast_to(at_ref[:, :ns], (batch, ns))
    a_im = jnp.broadcast_to(at_ref[:, ns:], (batch, ns))

    sub = 8
    per_tile = sub // batch

    def step(k, carry):
        h_re, h_im = carry
        r0 = pl.multiple_of(k * sub, sub)
        s_re = s_scr[pl.ds(r0, sub), :ns]
        s_im = s_scr[pl.ds(r0, sub), ns:]
        prev_re, prev_im = [], []
        for c in range(per_tile):
            rows = slice(c * batch, (c + 1) * batch)
            prev_re.append(h_re)
            prev_im.append(h_im)
            h_re, h_im = (a_re * h_re - a_im * h_im + s_re[rows],
                          a_re * h_im + a_im * h_re + s_im[rows])
        hp_scr[pl.ds(r0, sub), :ns] = jnp.concatenate(prev_re, axis=0)
        hp_scr[pl.ds(r0, sub), ns:] = jnp.concatenate(prev_im, axis=0)
        return h_re, h_im

    h_re, h_im = lax.fori_loop(0, n_chunks // per_tile, step,
                               (h_scr[:, :ns], h_scr[:, ns:]))
    h_scr[:, :ns] = h_re
    h_scr[:, ns:] = h_im
    y = (r[:, :out_w]
         + jnp.dot(hp_scr[...].astype(BF16), cs_ref[...], preferred_element_type=F32)
         + dsk_ref[...] * u)
    for t in range(SSM_T):
        y_ref[:, t, :] = y[:, t * LANES:(t + 1) * LANES]


def _ssm(u5, w_in, cs, a_t, dsk, n_rows=1024):
    lc, b, slabs, t, _ = u5.shape
    rows = lc * b
    width = t * LANES
    u4 = u5.reshape(rows, slabs, t, LANES)
    n_rows = min(n_rows, rows)
    kern = functools.partial(_ssm_kernel, batch=b, n_chunks=n_rows // b)
    y4 = pl.pallas_call(
        kern,
        grid=(slabs, rows // n_rows),
        in_specs=[
            pl.BlockSpec((n_rows, None, t, LANES), lambda j, i: (i, j, 0, 0)),
            pl.BlockSpec((t, LANES, width + 2 * SLAB_STATE), lambda j, i: (0, j, 0)),
            pl.BlockSpec((None, 2 * SLAB_STATE, width), lambda j, i: (j, 0, 0)),
            pl.BlockSpec((None, 1, 2 * SLAB_STATE), lambda j, i: (j, 0, 0)),
            pl.BlockSpec((None, 1, width), lambda j, i: (j, 0, 0)),
        ],
        out_specs=pl.BlockSpec((n_rows, None, t, LANES), lambda j, i: (i, j, 0, 0)),
        out_shape=jax.ShapeDtypeStruct((rows, slabs, t, LANES), F32),
        scratch_shapes=[
            pltpu.VMEM((n_rows, 2 * SLAB_STATE), F32),
            pltpu.VMEM((n_rows, 2 * SLAB_STATE), F32),
            pltpu.VMEM((b, 2 * SLAB_STATE), F32),
        ],
        compiler_params=_params(("parallel", "arbitrary")),
        name="ssm_scan",
    )(u4, w_in, cs, a_t, dsk)
    return y4.reshape(lc, b, slabs, t, LANES)


def _glu_out_kernel(y_ref, z_ref, wg_ref, bg_ref, wo_ref, x_ref, gate_ref, g_ref, o_ref):
    slabs = y_ref.shape[1]
    tm = z_ref.shape[0]
    half = tm // 2
    for r in range(2):
        rows = slice(r * half, (r + 1) * half)
        crow = slice(r * half // SSM_T, (r + 1) * half // SSM_T)
        y = jnp.concatenate([y_ref[crow, j].reshape(half, LANES) for j in range(slabs)], axis=-1)
        g = jax.nn.gelu(y)
        t = jnp.dot(g.astype(BF16), wg_ref[...], preferred_element_type=F32) + bg_ref[...]
        y2 = g * jax.nn.sigmoid(t) * jax.nn.silu(z_ref[rows, :].astype(F32))
        o = jnp.dot(y2.astype(BF16), wo_ref[...], preferred_element_type=F32)
        o_ref[rows, :] = x_ref[rows, :] + gate_ref[...] * _rms(o, g_ref[...])


def _glu_out(y5, z, w_glu, b_glu, w_out, x, gate, g_post, tm=512):
    b, l, d = x.shape
    _, _, slabs, t, _ = y5.shape
    width = slabs * LANES
    return pl.pallas_call(
        _glu_out_kernel,
        grid=(b, l // tm),
        in_specs=[
            pl.BlockSpec((tm // t, None, slabs, t, LANES), lambda bi, i: (i, bi, 0, 0, 0)),
            pl.BlockSpec((None, tm, width), lambda bi, i: (bi, i, 0)),
            pl.BlockSpec((width, width), lambda bi, i: (0, 0)),
            pl.BlockSpec((1, width), lambda bi, i: (0, 0)),
            pl.BlockSpec((width, d), lambda bi, i: (0, 0)),
            pl.BlockSpec((None, tm, d), lambda bi, i: (bi, i, 0)),
            pl.BlockSpec((None, 1, d), lambda bi, i: (bi, 0, 0)),
            pl.BlockSpec((1, d), lambda bi, i: (0, 0)),
        ],
        out_specs=pl.BlockSpec((None, tm, d), lambda bi, i: (bi, i, 0)),
        out_shape=jax.ShapeDtypeStruct((b, l, d), F32),
        compiler_params=_params(("parallel", "parallel")),
        name="glu_out",
    )(y5, z, w_glu, b_glu.reshape(1, width), w_out, x, gate.reshape(b, 1, d),
      g_post.reshape(1, d))


def _ab_layer(x, mod, g_pre, g_post, w_in, w_out, sgu_g, sgu_w, sgu_b):
    d = x.shape[-1]
    shift, scale, gate = mod[:, :d], mod[:, d:2 * d], mod[:, 2 * d:]
    wa = d // 2
    heads = wa // HEAD_DIM
    proj = _inproj(x, g_pre, scale, shift, w_in.astype(BF16))
    out_a = _sgu(proj, sgu_w, sgu_b, sgu_g, wa)
    attn_heads = math.gcd(heads, 8)
    blk = lambda cols: cols // (attn_heads * HEAD_DIM)
    out_b = _attention(proj, blk(3 * wa), blk(4 * wa), blk(5 * wa), blk(6 * wa), heads,
                       heads=attn_heads)
    return _ab_out(out_a, out_b, w_out.astype(BF16), x, gate, g_post)


def _ssm_layer(x, mod, g_pre, g_post, w_in, w_out, lam_re, lam_im, b_re, b_im, c_re, c_im,
               d_skip, log_dt, w_glu, b_glu):
    d = x.shape[-1]
    shift, scale, gate = mod[:, :d], mod[:, d:2 * d], mod[:, 2 * d:]
    u5, z = _inproj_ssm(x, g_pre, scale, shift, w_in.astype(BF16))
    y5 = _ssm(u5, *_ssm_prep(lam_re, lam_im, log_dt, b_re, b_im, c_re, c_im, d_skip))
    return _glu_out(y5, z, w_glu.astype(BF16), b_glu, w_out.astype(BF16), x, gate, g_post)


def kernel(x, c, ln_pre_g, ln_post_g, w_mod, b_mod, w_in_ab, w_out_ab, sgu_norm_g, sgu_w, sgu_b,
           w_in_ssm, w_out_ssm, lam_re, lam_im, b_re, b_im, c_re, c_im, d_skip, log_dt,
           w_glu, b_glu):
    depth = w_mod.shape[0]
    batch = x.shape[0]
    pad = (-batch) % 8
    c_pad = jnp.pad(c, ((0, pad), (0, 0)))
    mod = _adaln_mod(c_pad, w_mod, b_mod)[:, :batch]
    for layer in range(depth):
        i = layer // 2
        if layer % 2 == 0:
            x = _ab_layer(x, mod[layer], ln_pre_g[layer], ln_post_g[layer], w_in_ab[i],
                          w_out_ab[i], sgu_norm_g[i], sgu_w[i], sgu_b[i])
        else:
            x = _ssm_layer(x, mod[layer], ln_pre_g[layer], ln_post_g[layer], w_in_ssm[i],
                           w_out_ssm[i], lam_re[i], lam_im[i], b_re[i], b_im[i], c_re[i],
                           c_im[i], d_skip[i], log_dt[i], w_glu[i], b_glu[i])
    return x
```
